```python
import jax, jax.numpy as jnp
from jax import lax
import numpy as np

D_MODEL = 2048
BATCH = 1
SEQ = 8192
DEPTH = 1
DEC_BATCH = 32
DEC_SEQ = 4
PAST_LEN = 16384
PAGE_SIZE = 128

H_A = 16
DK_A = 128
DV_A = D_MODEL // H_A
CHUNK_A = 64
H_B = 16
DH_B = 128
Q_BLOCK = 128
FOX_BIAS_LO = 2.0
FOX_BIAS_HI = 10.0
N_EXPERTS = 64
TOP_K = 8
N_GROUPS = 8
TOPK_GROUPS = 4
D_EXPERT = 256
ROUTE_SCALE = 2.5
PLE_DIM = 256
ALPHA = (2.0 * DEPTH) ** 0.25
BETA = (8.0 * DEPTH) ** -0.25
LN_EPS = 1e-5
NEG_INF = -1e30
F32 = jnp.float32

IN_WIDTHS = (H_A * DK_A, H_A * DK_A, H_A * DV_A, H_A * DV_A,
             H_B * DH_B, H_B * DH_B, H_B * DH_B, H_B,
             D_MODEL, D_MODEL)
N_IN = sum(IN_WIDTHS)

kernel_name = 'hybrid_hgrn2_fox_moe_step'


def in_offsets():
    offs, acc = [], 0
    for w in IN_WIDTHS[:-1]:
        acc += w
        offs.append(acc)
    return offs


def layer_norm(x, g, b):
    xf = x.astype(F32)
    mu = jnp.mean(xf, axis=-1, keepdims=True)
    var = jnp.mean(jnp.square(xf - mu), axis=-1, keepdims=True)
    y = (xf - mu) * lax.rsqrt(var + LN_EPS)
    return (y * g.astype(F32) + b.astype(F32)).astype(x.dtype)


def hgrn2_scan(q, logf, k, v, s0):
    bsz, t_len, n_h, _ = q.shape
    dv = v.shape[-1]
    c = min(CHUNK_A, t_len)
    n_c = -(-t_len // c)
    pad = n_c * c - t_len

    def to_chunks(a):
        a = jnp.pad(a, ((0, 0), (0, pad), (0, 0), (0, 0)))
        return a.reshape(bsz, n_c, c, n_h, a.shape[-1]).transpose(1, 0, 3, 2, 4)

    mask = jnp.tril(jnp.ones((c, c), dtype=bool))[None, None, :, :, None]

    def step(s, xs):
        qc, gc, kc, vc = xs
        b = jnp.cumsum(gc, axis=2)
        o_inter = jnp.einsum('bhtk,bhkv->bhtv', qc * jnp.exp(b), s)
        diff = b[:, :, :, None, :] - b[:, :, None, :, :]
        decay = jnp.where(mask, jnp.exp(jnp.where(mask, diff, 0.0)), 0.0)
        att = jnp.einsum('bhtk,bhtsk,bhsk->bhts', qc, decay, kc)
        o = o_inter + jnp.einsum('bhts,bhsv->bhtv', att, vc)
        b_last = b[:, :, -1:, :]
        s_new = (jnp.exp(b_last[:, :, 0, :])[..., None] * s
                 + jnp.einsum('bhsk,bhsv->bhkv', kc * jnp.exp(b_last - b), vc))
        return s_new, o

    s_fin, o = lax.scan(step, s0, (to_chunks(q), to_chunks(logf), to_chunks(k), to_chunks(v)))
    o = o.transpose(1, 0, 3, 2, 4).reshape(bsz, n_c * c, n_h, dv)[:, :t_len]
    return o, s_fin


def hgrn2_branch(zq, zf, zi, zg, lb, norm_g, s0):
    bsz, t_len, _ = zq.shape
    shp = (bsz, t_len, H_A, DK_A)
    q = (jax.nn.silu(zq.astype(F32)) * DK_A ** -0.5).reshape(shp)
    zf = zf.astype(F32).reshape(shp)
    lb = lb.astype(F32).reshape(H_A, DK_A)
    logf = jnp.log(lb + (1.0 - lb) * jax.nn.sigmoid(zf))
    k = (1.0 - lb) * jax.nn.sigmoid(-zf)
    v = zi.astype(F32).reshape(bsz, t_len, H_A, DV_A)
    o, s_fin = hgrn2_scan(q, logf, k, v, s0.astype(F32))
    o = o * lax.rsqrt(jnp.mean(jnp.square(o), axis=-1, keepdims=True) + LN_EPS)
    o = o.reshape(bsz, t_len, H_A * DV_A) * norm_g.astype(F32) * jax.nn.silu(zg.astype(F32))
    return o.astype(zq.dtype), s_fin


def fox_prompt(q, k, v, lf):
    bsz, s_len, n_h, dh = q.shape
    nb = s_len // Q_BLOCK
    c = jnp.cumsum(lf, axis=1).transpose(0, 2, 1)
    qb = q.reshape(bsz, nb, Q_BLOCK, n_h, dh).transpose(1, 0, 2, 3, 4)
    cb = c.reshape(bsz, n_h, nb, Q_BLOCK).transpose(2, 0, 1, 3)
    kpos = jnp.arange(s_len)

    def block(args):
        i, qi, ci = args
        s = jnp.einsum('bthd,bshd->bhts', qi, k).astype(F32) + ci[..., :, None] - c[:, :, None, :]
        qpos = i * Q_BLOCK + jnp.arange(Q_BLOCK)
        s = jnp.where(kpos[None, :] <= qpos[:, None], s, NEG_INF)
        p = jax.nn.softmax(s, axis=-1)
        return jnp.einsum('bhts,bshd->bthd', p.astype(v.dtype), v)

    o = lax.map(block, (jnp.arange(nb), qb, cb))
    return o.transpose(1, 0, 2, 3, 4).reshape(bsz, s_len, n_h * dh)


def fox_sample(q, k_new, v_new, lf_new, cache_k, cache_v, cache_lf, page_table, layer):
    bd, t_len, n_h, dh = q.shape
    n_pages = page_table.shape[1]
    page = cache_k.shape[2]
    qf = q.astype(F32)
    c_new = jnp.cumsum(lf_new, axis=1).transpose(0, 2, 1)
    lf_past = cache_lf[layer, page_table].astype(F32).reshape(bd, n_pages * page, n_h)
    suffix = jnp.flip(jnp.cumsum(jnp.flip(lf_past, axis=1), axis=1), axis=1)
    c_past = (lf_past - suffix).reshape(bd, n_pages, page, n_h).transpose(1, 0, 3, 2)
    causal = jnp.tril(jnp.ones((t_len, t_len), dtype=bool))
    s_loc = jnp.einsum('bthd,bshd->bhts', qf, k_new.astype(F32)) + c_new[..., :, None] - c_new[..., None, :]
    s_loc = jnp.where(causal, s_loc, NEG_INF)
    m0 = jnp.max(s_loc, axis=-1, keepdims=True)
    p0 = jnp.exp(s_loc - m0)
    carry0 = (m0, jnp.sum(p0, axis=-1, keepdims=True),
              jnp.einsum('bhts,bshd->bhtd', p0, v_new.astype(F32)))

    def step(carry, xs):
        m, l_sum, acc = carry
        pt_col, cp = xs
        kp = cache_k[layer, pt_col].astype(F32)
        vp = cache_v[layer, pt_col].astype(F32)
        s = jnp.einsum('bthd,bshd->bhts', qf, kp) + c_new[..., :, None] - cp[:, :, None, :]
        m_new = jnp.maximum(m, jnp.max(s, axis=-1, keepdims=True))
        corr = jnp.exp(m - m_new)
        p = jnp.exp(s - m_new)
        return (m_new, l_sum * corr + jnp.sum(p, axis=-1, keepdims=True),
                acc * corr + jnp.einsum('bhts,bshd->bhtd', p, vp)), None

    (m, l_sum, acc), _ = lax.scan(step, carry0, (page_table.T, c_past))
    o = (acc / l_sum).transpose(0, 2, 1, 3).reshape(bd, t_len, n_h * dh)
    return o.astype(q.dtype)


def token_mixer(h, s0, fox_fn, l, W):
    z = jnp.einsum('btd,dn->btn', h, W['w_in'][l])
    (zq_a, zf_a, zi_a, zg_a, zq_b, zk_b, zv_b, zf_b, zga, zgb) = jnp.split(z, in_offsets(), axis=-1)
    ya, s_new = hgrn2_branch(zq_a, zf_a, zi_a, zg_a, W['lb'][l], W['hgrn_norm_g'][l], s0)
    bsz, t_len, _ = h.shape
    shp = (bsz, t_len, H_B, DH_B)
    q = zq_b.reshape(shp) * DH_B ** -0.5
    k = zk_b.reshape(shp)
    v = zv_b.reshape(shp)
    lf = jax.nn.log_sigmoid((zf_b + W['b_fox_f'][l]).astype(F32))
    yb = fox_fn(q, k, v, lf)
    merged = (jax.nn.sigmoid(zga + W['b_merge'][l, 0]) * (ya @ W['w_proj_a'][l])
              + jax.nn.sigmoid(zgb + W['b_merge'][l, 1]) * (yb @ W['w_proj_b'][l]))
    return merged @ W['w_out'][l], (k, v, lf, s_new)


def moe(u, l, W):
    t = u.reshape(-1, D_MODEL)
    n_tok = t.shape[0]
    scores = jax.nn.sigmoid((t @ W['w_router'][l]).astype(F32))
    sel = scores + W['b_router'][l].astype(F32)
    grp = sel.reshape(n_tok, N_GROUPS, N_EXPERTS // N_GROUPS)
    gscore = jnp.sum(lax.top_k(grp, 2)[0], axis=-1)
    _, gidx = lax.top_k(gscore, TOPK_GROUPS)
    gmask = jnp.sum(jax.nn.one_hot(gidx, N_GROUPS, dtype=F32), axis=1) > 0
    emask = jnp.repeat(gmask, N_EXPERTS // N_GROUPS, axis=-1)
    _, idx = lax.top_k(jnp.where(emask, sel, NEG_INF), TOP_K)
    w = jnp.take_along_axis(scores, idx, axis=-1)
    w = w / jnp.sum(w, axis=-1, keepdims=True) * ROUTE_SCALE
    combine = jnp.sum(jax.nn.one_hot(idx, N_EXPERTS, dtype=F32) * w[..., None], axis=1).astype(t.dtype)
    hg = jnp.einsum('td,edf->tef', t, W['w_exp_gate'][l])
    hu = jnp.einsum('td,edf->tef', t, W['w_exp_up'][l])
    routed = jnp.einsum('tef,efd->td', jax.nn.silu(hg) * hu * combine[:, :, None], W['w_exp_down'][l])
    shared = (jax.nn.silu(t @ W['w_sh_gate'][l]) * (t @ W['w_sh_up'][l])) @ W['w_sh_down'][l]
    return (routed + shared).reshape(u.shape)


def trunk_layer(x, pe, s0, fox_fn, l, W):
    mix, st = token_mixer(x, s0, fox_fn, l, W)
    u = layer_norm(ALPHA * x + mix, W['ln1_g'][l], W['ln1_b'][l])
    ple = jax.nn.sigmoid(u @ W['w_ple_gate'][l]) * (pe @ W['w_ple_proj'][l])
    y = layer_norm(ALPHA * u + moe(u, l, W) + ple, W['ln2_g'][l], W['ln2_b'][l])
    return y, st


def setup_inputs(seed: int = 0) -> dict:
    key = jax.random.key(seed)
    ks = jax.random.split(key, 32)
    n_pages = PAST_LEN // PAGE_SIZE
    n_pool = (DEC_BATCH * n_pages * 5) // 4

    def nrm(k, shape, scale):
        return jax.random.normal(k, shape, F32) * scale

    head_bias = jnp.linspace(FOX_BIAS_LO, FOX_BIAS_HI, H_B, dtype=F32)
    page_table = jax.random.permutation(ks[8], n_pool)[: DEC_BATCH * n_pages].reshape(DEC_BATCH, n_pages).astype(jnp.int32)
    return {
        'x_prompt': nrm(ks[0], (BATCH, SEQ, D_MODEL), 1.0),
        'x_sample': nrm(ks[1], (DEC_BATCH, DEC_SEQ, D_MODEL), 1.0),
        'p_prompt': nrm(ks[2], (DEPTH, BATCH, SEQ, PLE_DIM), 1.0),
        'p_sample': nrm(ks[3], (DEPTH, DEC_BATCH, DEC_SEQ, PLE_DIM), 1.0),
        'cache_k': nrm(ks[4], (DEPTH, n_pool, PAGE_SIZE, H_B, DH_B), 1.0),
        'cache_v': nrm(ks[5], (DEPTH, n_pool, PAGE_SIZE, H_B, DH_B), 1.0),
        'cache_lf': jax.nn.log_sigmoid(head_bias + nrm(ks[6], (DEPTH, n_pool, PAGE_SIZE, H_B), 0.5)),
        'state_hgrn': nrm(ks[7], (DEPTH, DEC_BATCH, H_A, DK_A, DV_A), 0.5),
        'page_table': page_table,
        'w_in': nrm(ks[9], (DEPTH, D_MODEL, N_IN), D_MODEL ** -0.5),
        'b_fox_f': head_bias + nrm(ks[10], (DEPTH, H_B), 0.1),
        'b_merge': nrm(ks[11], (DEPTH, 2, D_MODEL), 0.1),
        'lb_param': nrm(ks[12], (DEPTH + 1, H_A * DK_A), 0.5),
        'hgrn_norm_g': 1.0 + nrm(ks[13], (DEPTH, H_A * DV_A), 0.1),
        'w_proj_a': nrm(ks[14], (DEPTH, H_A * DV_A, D_MODEL), BETA * (H_A * DV_A) ** -0.5),
        'w_proj_b': nrm(ks[15], (DEPTH, H_B * DH_B, D_MODEL), BETA * (H_B * DH_B) ** -0.5),
        'w_out': nrm(ks[16], (DEPTH, D_MODEL, D_MODEL), BETA * D_MODEL ** -0.5),
        'ln1_g': 1.0 + nrm(ks[17], (DEPTH, D_MODEL), 0.1),
        'ln1_b': nrm(ks[18], (DEPTH, D_MODEL), 0.1),
        'ln2_g': 1.0 + nrm(ks[19], (DEPTH, D_MODEL), 0.1),
        'ln2_b': nrm(ks[20], (DEPTH, D_MODEL), 0.1),
        'w_router': nrm(ks[21], (DEPTH, D_MODEL, N_EXPERTS), D_MODEL ** -0.5),
        'b_router': nrm(ks[22], (DEPTH, N_EXPERTS), 0.01),
        'w_exp_gate': nrm(ks[23], (DEPTH, N_EXPERTS, D_MODEL, D_EXPERT), D_MODEL ** -0.5),
        'w_exp_up': nrm(ks[24], (DEPTH, N_EXPERTS, D_MODEL, D_EXPERT), D_MODEL ** -0.5),
        'w_exp_down': nrm(ks[25], (DEPTH, N_EXPERTS, D_EXPERT, D_MODEL), BETA * D_EXPERT ** -0.5),
        'w_sh_gate': nrm(ks[26], (DEPTH, D_MODEL, D_EXPERT), D_MODEL ** -0.5),
        'w_sh_up': nrm(ks[27], (DEPTH, D_MODEL, D_EXPERT), D_MODEL ** -0.5),
        'w_sh_down': nrm(ks[28], (DEPTH, D_EXPERT, D_MODEL), BETA * D_EXPERT ** -0.5),
        'w_ple_proj': nrm(ks[29], (DEPTH, PLE_DIM, D_MODEL), BETA * PLE_DIM ** -0.5),
        'w_ple_gate': nrm(ks[30], (DEPTH, D_MODEL, D_MODEL), D_MODEL ** -0.5),
    }


def reference(x_prompt, x_sample, p_prompt, p_sample, cache_k, cache_v, cache_lf, state_hgrn, page_table,
              w_in, b_fox_f, b_merge, lb_param, hgrn_norm_g, w_proj_a, w_proj_b, w_out,
              ln1_g, ln1_b, ln2_g, ln2_b, w_router, b_router, w_exp_gate, w_exp_up, w_exp_down,
              w_sh_gate, w_sh_up, w_sh_down, w_ple_proj, w_ple_gate):
    lb = jnp.cumsum(jax.nn.softmax(lb_param.astype(F32), axis=0), axis=0)
    W = dict(w_in=w_in, b_fox_f=b_fox_f, b_merge=b_merge, lb=lb, hgrn_norm_g=hgrn_norm_g,
             w_proj_a=w_proj_a, w_proj_b=w_proj_b, w_out=w_out, ln1_g=ln1_g, ln1_b=ln1_b,
             ln2_g=ln2_g, ln2_b=ln2_b, w_router=w_router, b_router=b_router, w_exp_gate=w_exp_gate,
             w_exp_up=w_exp_up, w_exp_down=w_exp_down, w_sh_gate=w_sh_gate, w_sh_up=w_sh_up,
             w_sh_down=w_sh_down, w_ple_proj=w_ple_proj, w_ple_gate=w_ple_gate)
    xp, xs = x_prompt, x_sample
    kp_l, vp_l, lfp_l, sp_l = [], [], [], []
    ks_l, vs_l, lfs_l, ss_l = [], [], [], []
    for l in range(DEPTH):
        s0p = jnp.zeros((x_prompt.shape[0], H_A, DK_A, DV_A), F32)
        xp, (kp, vp, lfp, sp) = trunk_layer(xp, p_prompt[l], s0p, fox_prompt, l, W)

        def fox_s(q, k, v, lf, l=l):
            return fox_sample(q, k, v, lf, cache_k, cache_v, cache_lf, page_table, l)

        xs, (ks_, vs_, lfs, ss) = trunk_layer(xs, p_sample[l], state_hgrn[l], fox_s, l, W)
        kp_l.append(kp); vp_l.append(vp); lfp_l.append(lfp); sp_l.append(sp)
        ks_l.append(ks_); vs_l.append(vs_); lfs_l.append(lfs); ss_l.append(ss)
    y_prompt = xp
    y_sample = xs
    k_prompt = jnp.stack(kp_l, axis=0)
    v_prompt = jnp.stack(vp_l, axis=0)
    lf_prompt = jnp.stack(lfp_l, axis=0)
    s_prompt = jnp.stack(sp_l, axis=0)
    k_sample = jnp.stack(ks_l, axis=0)
    v_sample = jnp.stack(vs_l, axis=0)
    lf_sample = jnp.stack(lfs_l, axis=0)
    s_sample = jnp.stack(ss_l, axis=0)
    return (y_prompt, y_sample, k_prompt, v_prompt, lf_prompt, s_prompt, k_sample, v_sample, lf_sample, s_sample)
```

```python
import functools

import jax
import jax.numpy as jnp
from jax import lax
from jax.experimental import pallas as pl
from jax.experimental.pallas import tpu as pltpu

F32 = jnp.float32
BF16 = jnp.bfloat16

LANES = 128
VMEM_LIMIT = 56 * 1024 * 1024

LN_EPS = 1e-5
NEG_INF = -1e30
ROUTE_SCALE = 2.5
N_GROUPS = 8
TOPK_GROUPS = 4
TOP_K = 8
HGRN_BLK = 16

HI = lax.Precision.HIGHEST


def _cparams(*sem):
    return pltpu.CompilerParams(dimension_semantics=sem, vmem_limit_bytes=VMEM_LIMIT)


def _sigmoid(x):
    return 1.0 / (1.0 + jnp.exp(-x))


def _silu(x):
    return x * _sigmoid(x)


def _log_sigmoid(x):
    return jnp.minimum(x, 0.0) - jnp.log1p(jnp.exp(-jnp.abs(x)))


def _dot_nt(a, b, precision=None):
    return lax.dot_general(a, b, (((1,), (1,)), ((), ())), preferred_element_type=F32, precision=precision)


def _dot_tn(a, b, precision=None):
    return lax.dot_general(a, b, (((0,), (0,)), ((), ())), preferred_element_type=F32, precision=precision)


def _mm_kernel(x_ref, w_ref, o_ref):
    o_ref[...] = jnp.dot(x_ref[...], w_ref[...], preferred_element_type=F32).astype(o_ref.dtype)


def matmul(x, w, tm, tn, out_dtype=F32):
    m, k = x.shape
    n = w.shape[1]
    return pl.pallas_call(
        _mm_kernel,
        grid=(m // tm, n // tn),
        in_specs=[pl.BlockSpec((tm, k), lambda i, j: (i, 0)),
                  pl.BlockSpec((k, tn), lambda i, j: (0, j))],
        out_specs=pl.BlockSpec((tm, tn), lambda i, j: (i, j)),
        out_shape=jax.ShapeDtypeStruct((m, n), out_dtype),
        compiler_params=_cparams("parallel", "parallel"),
    )(x, w)


def _hgrn_head(zq, zf, zi, zg, lb, ng, st, lmat, bmat, valid, dk):
    rows = zq.shape[0]
    q = _silu(zq) * dk ** -0.5
    logf = jnp.log(lb + (1.0 - lb) * _sigmoid(zf))
    k = (1.0 - lb) * _sigmoid(-zf)
    if valid is not None:
        logf = jnp.where(valid, logf, 0.0)
        k = jnp.where(valid, k, 0.0)
    b = jnp.dot(lmat, logf, preferred_element_type=F32, precision=HI)
    blast = jnp.dot(bmat, logf, preferred_element_type=F32, precision=HI)
    qt = (q * jnp.exp(b)).astype(BF16)
    kt = (k * jnp.exp(blast - b)).astype(BF16)
    vb = zi.astype(BF16)
    srow = lax.broadcasted_iota(jnp.int32, (HGRN_BLK, 1), 0)
    outs = []
    for i in range(rows // HGRN_BLK):
        sl = slice(i * HGRN_BLK, (i + 1) * HGRN_BLK)
        o_blk = _dot_nt(qt[sl], st.astype(BF16))
        qb, kb, bb, vv = q[sl], k[sl], b[sl], zi[sl]
        for t in range(HGRN_BLK):
            e = jnp.exp(jnp.minimum(bb[t:t + 1] - bb, 0.0))
            col = jnp.sum(qb[t:t + 1] * kb * e, axis=1, keepdims=True)
            col = jnp.where(srow <= t, col, 0.0)
            o_t = jnp.sum(col * vv, axis=0, keepdims=True)
            o_blk = o_blk + jnp.where(srow == t, o_t, 0.0)
        outs.append(o_blk)
        decay = jnp.exp(blast[i * HGRN_BLK:i * HGRN_BLK + 1])
        st = st * decay + _dot_tn(vb[sl], kt[sl])
    o = jnp.concatenate(outs, axis=0)
    o = o * lax.rsqrt(jnp.mean(jnp.square(o), axis=1, keepdims=True) + LN_EPS)
    return o * ng * _silu(zg), st


def _block_mats(rows):
    r = lax.broadcasted_iota(jnp.int32, (rows, rows), 0)
    c = lax.broadcasted_iota(jnp.int32, (rows, rows), 1)
    same = (r // HGRN_BLK) == (c // HGRN_BLK)
    lmat = jnp.where(same & (c <= r), 1.0, 0.0).astype(F32)
    bmat = jnp.where(same, 1.0, 0.0).astype(F32)
    return lmat, bmat


def _lower_bound(lbp_ref, sl):
    p = lbp_ref[:, sl]
    m = jnp.max(p, axis=0, keepdims=True)
    e = jnp.exp(p - m)
    return e[0:1] / jnp.sum(e, axis=0, keepdims=True)


def _hgrn_prompt_kernel(zq_ref, zf_ref, zi_ref, zg_ref, lbp_ref, ng_ref, y_ref, s_ref, st_ref, *, n_heads, dk):
    step = pl.program_id(0)

    @pl.when(step == 0)
    def _():
        st_ref[...] = jnp.zeros_like(st_ref)

    lmat, bmat = _block_mats(zq_ref.shape[0])

    def head(h, carry):
        sl = pl.ds(pl.multiple_of(h * LANES, LANES), LANES)
        y, st = _hgrn_head(zq_ref[:, sl], zf_ref[:, sl], zi_ref[:, sl], zg_ref[:, sl],
                           _lower_bound(lbp_ref, sl), ng_ref[:, sl], st_ref[h], lmat, bmat, None, dk)
        y_ref[:, sl] = y.astype(y_ref.dtype)
        st_ref[h] = st
        return carry

    lax.fori_loop(0, n_heads, head, 0)

    @pl.when(step == pl.num_programs(0) - 1)
    def _():
        for h in range(n_heads):
            s_ref[h] = st_ref[h].T


def hgrn_prompt(z, lb_param, norm_g, t_len, n_heads, rows):
    width = n_heads * LANES
    kern = functools.partial(_hgrn_prompt_kernel, n_heads=n_heads, dk=LANES)
    return pl.pallas_call(
        kern,
        grid=(t_len // rows,),
        in_specs=[pl.BlockSpec((rows, width), lambda i: (i, 0)),
                  pl.BlockSpec((rows, width), lambda i: (i, 1)),
                  pl.BlockSpec((rows, width), lambda i: (i, 2)),
                  pl.BlockSpec((rows, width), lambda i: (i, 3)),
                  pl.BlockSpec((2, width), lambda i: (0, 0)),
                  pl.BlockSpec((1, width), lambda i: (0, 0))],
        out_specs=[pl.BlockSpec((rows, width), lambda i: (i, 0)),
                   pl.BlockSpec((n_heads, LANES, LANES), lambda i: (0, 0, 0))],
        out_shape=[jax.ShapeDtypeStruct((t_len, width), BF16),
                   jax.ShapeDtypeStruct((n_heads, LANES, LANES), F32)],
        scratch_shapes=[pltpu.VMEM((n_heads, LANES, LANES), F32)],
        compiler_params=_cparams("arbitrary"),
    )(z, z, z, z, lb_param, norm_g)


def _hgrn_sample_kernel(zq_ref, zf_ref, zi_ref, zg_ref, lbp_ref, ng_ref, s0_ref, y_ref, s_ref, *, n_heads, dk, t_new):
    rows = zq_ref.shape[0]
    lmat, bmat = _block_mats(rows)
    valid = lax.broadcasted_iota(jnp.int32, (rows, 1), 0) < t_new

    def head(h, carry):
        sl = pl.ds(pl.multiple_of(h * LANES, LANES), LANES)
        y, st = _hgrn_head(zq_ref[:, sl], zf_ref[:, sl], zi_ref[:, sl], zg_ref[:, sl],
                           _lower_bound(lbp_ref, sl), ng_ref[:, sl], s0_ref[0, h].T, lmat, bmat, valid, dk)
        y_ref[:, sl] = y.astype(y_ref.dtype)
        s_ref[0, h] = st.T
        return carry

    lax.fori_loop(0, n_heads, head, 0)


def hgrn_sample(zs, lb_param, norm_g, s0, n_heads, t_new):
    bsz = s0.shape[0]
    width = n_heads * LANES
    kern = functools.partial(_hgrn_sample_kernel, n_heads=n_heads, dk=LANES, t_new=t_new)
    return pl.pallas_call(
        kern,
        grid=(bsz,),
        in_specs=[pl.BlockSpec((HGRN_BLK, width), lambda b: (b, 0)),
                  pl.BlockSpec((HGRN_BLK, width), lambda b: (b, 1)),
                  pl.BlockSpec((HGRN_BLK, width), lambda b: (b, 2)),
                  pl.BlockSpec((HGRN_BLK, width), lambda b: (b, 3)),
                  pl.BlockSpec((2, width), lambda b: (0, 0)),
                  pl.BlockSpec((1, width), lambda b: (0, 0)),
                  pl.BlockSpec((1, n_heads, LANES, LANES), lambda b: (b, 0, 0, 0))],
        out_specs=[pl.BlockSpec((HGRN_BLK, width), lambda b: (b, 0)),
                   pl.BlockSpec((1, n_heads, LANES, LANES), lambda b: (b, 0, 0, 0))],
        out_shape=[jax.ShapeDtypeStruct((bsz * HGRN_BLK, width), BF16),
                   jax.ShapeDtypeStruct(s0.shape, F32)],
        compiler_params=_cparams("parallel"),
    )(zs, zs, zs, zs, lb_param, norm_g, s0)


def _fox_gate_prompt_kernel(zf_ref, bias_ref, lf_ref, ct_ref, carry_ref):
    @pl.when(pl.program_id(0) == 0)
    def _():
        carry_ref[...] = jnp.zeros_like(carry_ref)

    lf = _log_sigmoid(zf_ref[...] + bias_ref[...])
    lf_ref[...] = lf
    rows = lf.shape[0]
    r = lax.broadcasted_iota(jnp.int32, (rows, rows), 0)
    c = lax.broadcasted_iota(jnp.int32, (rows, rows), 1)
    upper = jnp.where(r <= c, 1.0, 0.0).astype(F32)
    ct = _dot_tn(lf, upper, precision=HI) + carry_ref[:, 0:1]
    ct_ref[...] = ct
    carry_ref[...] = jnp.broadcast_to(ct[:, rows - 1:rows], carry_ref.shape)


def fox_gate_prompt(zf, bias, t_len, rows):
    return pl.pallas_call(
        _fox_gate_prompt_kernel,
        grid=(t_len // rows,),
        in_specs=[pl.BlockSpec((rows, LANES), lambda i: (i, 0)),
                  pl.BlockSpec((1, LANES), lambda i: (0, 0))],
        out_specs=[pl.BlockSpec((rows, LANES), lambda i: (i, 0)),
                   pl.BlockSpec((LANES, rows), lambda i: (0, i))],
        out_shape=[jax.ShapeDtypeStruct((t_len, LANES), F32),
                   jax.ShapeDtypeStruct((LANES, t_len), F32)],
        scratch_shapes=[pltpu.VMEM((LANES, LANES), F32)],
        compiler_params=_cparams("arbitrary"),
    )(zf, bias)


def _fox_gate_sample_kernel(zf_ref, bias_ref, lf_ref, c_ref, *, t_new):
    lf = _log_sigmoid(zf_ref[...] + bias_ref[...])
    lf_ref[...] = lf
    rows = lf.shape[0]
    r = lax.broadcasted_iota(jnp.int32, (rows, rows), 0)
    c = lax.broadcasted_iota(jnp.int32, (rows, rows), 1)
    tri = jnp.where(((r // t_new) == (c // t_new)) & (c <= r), 1.0, 0.0).astype(F32)
    c_ref[...] = jnp.dot(tri, lf, preferred_element_type=F32, precision=HI)


def fox_gate_sample(zf, bias, t_new):
    rows = zf.shape[0]
    return pl.pallas_call(
        functools.partial(_fox_gate_sample_kernel, t_new=t_new),
        grid=(1,),
        in_specs=[pl.BlockSpec((rows, LANES), lambda i: (0, 0)),
                  pl.BlockSpec((1, LANES), lambda i: (0, 0))],
        out_specs=[pl.BlockSpec((rows, LANES), lambda i: (0, 0)),
                   pl.BlockSpec((rows, LANES), lambda i: (0, 0))],
        out_shape=[jax.ShapeDtypeStruct((rows, LANES), F32),
                   jax.ShapeDtypeStruct((rows, LANES), F32)],
        compiler_params=_cparams("arbitrary"),
    )(zf, bias)


def _fox_prompt_kernel(q_ref, k_ref, v_ref, c_ref, o_ref, kb_ref, vb_ref, *, blk, scale):
    i = pl.program_id(1)

    @pl.when(i == 0)
    def _():
        kb_ref[...] = k_ref[...].astype(BF16)
        vb_ref[...] = v_ref[...].astype(BF16)

    q = (q_ref[...] * scale).astype(BF16)
    c0 = c_ref[0, :, pl.ds(pl.multiple_of(i * blk, blk), LANES)][:, 0:1]

    def scores(j):
        off = pl.multiple_of(j * blk, blk)
        s = _dot_nt(q, kb_ref[pl.ds(off, blk), :])
        return s + (c0 - c_ref[0, :, pl.ds(off, blk)])

    def update(j, s, carry):
        m, l, acc = carry
        m_new = jnp.maximum(m, jnp.max(s, axis=1, keepdims=True))
        alpha = jnp.exp(m - m_new)
        p = jnp.exp(s - m_new)
        l = alpha * l + jnp.sum(p, axis=1, keepdims=True)
        off = pl.multiple_of(j * blk, blk)
        acc = alpha * acc + jnp.dot(p.astype(BF16), vb_ref[pl.ds(off, blk), :], preferred_element_type=F32)
        return m_new, l, acc

    init = (jnp.full((blk, 1), NEG_INF, F32), jnp.zeros((blk, 1), F32), jnp.zeros((blk, LANES), F32))
    carry = lax.fori_loop(0, i, lambda j, c: update(j, scores(j), c), init)
    row = lax.broadcasted_iota(jnp.int32, (blk, blk), 0)
    col = lax.broadcasted_iota(jnp.int32, (blk, blk), 1)
    _, l, acc = update(i, jnp.where(col <= row, scores(i), NEG_INF), carry)
    o_ref[...] = (acc / l).astype(o_ref.dtype)


def fox_prompt(z, ct, t_len, n_heads, q_col, k_col, v_col, blk):
    kern = functools.partial(_fox_prompt_kernel, blk=blk, scale=LANES ** -0.5)
    return pl.pallas_call(
        kern,
        grid=(n_heads, t_len // blk),
        in_specs=[pl.BlockSpec((blk, LANES), lambda h, i: (i, q_col + h)),
                  pl.BlockSpec((t_len, LANES), lambda h, i: (0, k_col + h)),
                  pl.BlockSpec((t_len, LANES), lambda h, i: (0, v_col + h)),
                  pl.BlockSpec((1, 1, t_len), lambda h, i: (h, 0, 0))],
        out_specs=pl.BlockSpec((blk, LANES), lambda h, i: (i, h)),
        out_shape=jax.ShapeDtypeStruct((t_len, n_heads * LANES), BF16),
        scratch_shapes=[pltpu.VMEM((t_len, LANES), BF16), pltpu.VMEM((t_len, LANES), BF16)],
        compiler_params=_cparams("arbitrary", "arbitrary"),
    )(z, z, z, ct)


HEAD_GROUP = 8


def _cache_bias_kernel(lf_ref, o_ref):
    page = lf_ref.shape[2]
    ks = lax.broadcasted_iota(jnp.int32, (page, 2 * page), 0)
    kj = lax.broadcasted_iota(jnp.int32, (page, 2 * page), 1)
    later = jnp.where(ks > kj, 1.0, jnp.where(kj >= page, 1.0, 0.0)).astype(F32)

    def one(p, carry):
        o_ref[p] = _dot_tn(lf_ref[0, p], later, precision=HI)
        return carry

    lax.fori_loop(0, lf_ref.shape[1], one, 0)


def fox_cache_bias(cache_lf, layer, pages_per_step):
    depth, pool, page, n_heads = cache_lf.shape
    return pl.pallas_call(
        _cache_bias_kernel,
        grid=(pool // pages_per_step,),
        in_specs=[pl.BlockSpec((1, pages_per_step, page, n_heads), lambda i: (layer, i, 0, 0))],
        out_specs=pl.BlockSpec((pages_per_step, n_heads, 2 * page), lambda i: (i, 0, 0)),
        out_shape=jax.ShapeDtypeStruct((pool, n_heads, 2 * page), F32),
        compiler_params=_cparams("parallel"),
    )(cache_lf)


def _fox_sample_kernel(pt_ref, q_ref, cnew_ref, cnk_ref, kn_ref, vn_ref, *rest, n_slots, t_new):
    k_refs, v_refs, b_refs = rest[:n_slots], rest[n_slots:2 * n_slots], rest[2 * n_slots:3 * n_slots]
    o_ref = rest[3 * n_slots]
    m_ref, l_ref, acc_ref, carry_ref = rest[3 * n_slots + 1:]
    step = pl.program_id(1)
    n_groups = q_ref.shape[1]
    grows = q_ref.shape[2]
    page = k_refs[0].shape[2]

    def own_head(n_keys):
        r = lax.broadcasted_iota(jnp.int32, (grows, n_keys * HEAD_GROUP), 0)
        c = lax.broadcasted_iota(jnp.int32, (grows, n_keys * HEAD_GROUP), 1)
        return r, c, (c % HEAD_GROUP) == (r // t_new)

    def tile(ref, lead, g):
        x = ref[lead + (slice(None), slice(g * HEAD_GROUP, (g + 1) * HEAD_GROUP), slice(None))]
        return x.reshape(x.shape[0] * HEAD_GROUP, x.shape[2]).astype(BF16)

    @pl.when(step == 0)
    def _():
        r, c, same = own_head(2 * t_new)
        visible = same & ((c // HEAD_GROUP) <= (r % t_new))
        for g in range(n_groups):
            s = _dot_nt(q_ref[0, g].astype(BF16), tile(kn_ref, (0,), g))
            s = s + jnp.where(visible, cnew_ref[0, g] - cnk_ref[0, g], NEG_INF)
            m0 = jnp.max(s, axis=1, keepdims=True)
            p0 = jnp.exp(s - m0)
            m_ref[g] = m0
            l_ref[g] = jnp.sum(p0, axis=1, keepdims=True)
            acc_ref[g] = jnp.dot(p0.astype(BF16), tile(vn_ref, (0,), g), preferred_element_type=F32)
        carry_ref[...] = jnp.zeros_like(carry_ref)

    _, _, same = own_head(page)
    for g in range(n_groups):
        qg = q_ref[0, g].astype(BF16)
        cn = cnew_ref[0, g]
        m, l, acc, carry = m_ref[g], l_ref[g], acc_ref[g], carry_ref[g]
        for r in range(n_slots):
            s = _dot_nt(qg, tile(k_refs[r], (0, 0), g))
            s = s + jnp.where(same, cn + (carry + b_refs[r][0, g, 0:1, :]), NEG_INF)
            carry = carry + b_refs[r][0, g, 1:2, :]
            m_new = jnp.maximum(m, jnp.max(s, axis=1, keepdims=True))
            alpha = jnp.exp(m - m_new)
            p = jnp.exp(s - m_new)
            l = alpha * l + jnp.sum(p, axis=1, keepdims=True)
            acc = alpha * acc + jnp.dot(p.astype(BF16), tile(v_refs[r], (0, 0), g), preferred_element_type=F32)
            m = m_new
        m_ref[g], l_ref[g], acc_ref[g], carry_ref[g] = m, l, acc, carry

    @pl.when(step == pl.num_programs(1) - 1)
    def _():
        for g in range(n_groups):
            o_ref[0, g] = acc_ref[g] / l_ref[g]


def fox_sample(page_table, q, cnew, cnk, k_new, v_new, cache_k, cache_v, bias_rows, layer, n_slots, t_new):
    bsz, n_pages = page_table.shape
    _, n_groups, grows, _ = q.shape
    _, _, page, n_heads, dh = cache_k.shape
    cols = page * HEAD_GROUP

    def page_idx(b, g, pt, r):
        return pt[b, n_pages - 1 - (g * n_slots + r)]

    def per_batch(shape):
        return pl.BlockSpec((1,) + shape, lambda b, g, pt: (b,) + (0,) * len(shape))

    in_specs = [per_batch((n_groups, grows, dh)), per_batch((n_groups, grows, 1)),
                per_batch((n_groups, 1, 2 * t_new * HEAD_GROUP)),
                per_batch((2 * t_new, n_heads, dh)), per_batch((2 * t_new, n_heads, dh))]
    for _ in range(2):
        in_specs += [pl.BlockSpec((1, 1, page, n_heads, dh),
                                  functools.partial(lambda b, g, pt, r: (layer, page_idx(b, g, pt, r), 0, 0, 0), r=r))
                     for r in range(n_slots)]
    in_specs += [pl.BlockSpec((1, n_groups, 2, cols),
                              functools.partial(lambda b, g, pt, r: (page_idx(b, g, pt, r), 0, 0, 0), r=r))
                 for r in range(n_slots)]
    grid_spec = pltpu.PrefetchScalarGridSpec(
        num_scalar_prefetch=1,
        grid=(bsz, n_pages // n_slots),
        in_specs=in_specs,
        out_specs=per_batch((n_groups, grows, dh)),
        scratch_shapes=[pltpu.VMEM((n_groups, grows, 1), F32), pltpu.VMEM((n_groups, grows, 1), F32),
                        pltpu.VMEM((n_groups, grows, dh), F32), pltpu.VMEM((n_groups, 1, cols), F32)],
    )
    kern = functools.partial(_fox_sample_kernel, n_slots=n_slots, t_new=t_new)
    return pl.pallas_call(
        kern,
        grid_spec=grid_spec,
        out_shape=jax.ShapeDtypeStruct(q.shape, F32),
        compiler_params=_cparams("arbitrary", "arbitrary"),
    )(page_table, q, cnew, cnk, k_new, v_new,
      *([cache_k] * n_slots), *([cache_v] * n_slots), *([bias_rows] * n_slots))


def fox_sample_wrapper(page_table, zq, zk, zv, c_new, cache_k, cache_v, cache_lf, layer, bsz, t_new, n_slots,
                       bias_pages_per_step):
    _, pool, page, n_heads, dh = cache_k.shape
    n_groups = n_heads // HEAD_GROUP
    grows = HEAD_GROUP * t_new
    q = (zq * dh ** -0.5).reshape(bsz, t_new, n_heads, dh).transpose(0, 2, 1, 3).reshape(bsz, n_groups, grows, dh)
    c = c_new[:, :n_heads].reshape(bsz, t_new, n_groups, HEAD_GROUP)
    cnew = c.transpose(0, 2, 3, 1).reshape(bsz, n_groups, grows, 1)
    cnk = jnp.pad(c.transpose(0, 2, 1, 3), ((0, 0), (0, 0), (0, t_new), (0, 0)))
    cnk = cnk.reshape(bsz, n_groups, 1, 2 * t_new * HEAD_GROUP)
    pad = ((0, 0), (0, t_new), (0, 0), (0, 0))
    k_new = jnp.pad(zk.reshape(bsz, t_new, n_heads, dh), pad)
    v_new = jnp.pad(zv.reshape(bsz, t_new, n_heads, dh), pad)
    bias = fox_cache_bias(cache_lf, layer, bias_pages_per_step)
    bias_rows = bias.reshape(pool, n_groups, HEAD_GROUP, 2, page).transpose(0, 1, 3, 4, 2)
    bias_rows = bias_rows.reshape(pool, n_groups, 2, page * HEAD_GROUP)
    o = fox_sample(page_table, q, cnew, cnk, k_new, v_new, cache_k, cache_v, bias_rows, layer, n_slots, t_new)
    return o.reshape(bsz, n_heads, t_new, dh).transpose(0, 2, 1, 3).reshape(bsz * t_new, n_heads * dh)


def _layer_norm(h, g, b):
    mu = jnp.mean(h, axis=1, keepdims=True)
    d = h - mu
    var = jnp.mean(jnp.square(d), axis=1, keepdims=True)
    return d * lax.rsqrt(var + LN_EPS) * g + b


def _merge_kernel(ya_ref, yb_ref, wa_ref, wb_ref, ga_ref, gb_ref, bm_ref, o_ref):
    a = jnp.dot(ya_ref[...], wa_ref[...], preferred_element_type=F32)
    b = jnp.dot(yb_ref[...], wb_ref[...], preferred_element_type=F32)
    o = _sigmoid(ga_ref[...] + bm_ref[0:1]) * a + _sigmoid(gb_ref[...] + bm_ref[1:2]) * b
    o_ref[...] = o.astype(o_ref.dtype)


def merge_branches(ya, yb, wa, wb, z, ga_col, gb_col, b_merge, tm, tn):
    m, k = ya.shape
    n = wa.shape[1]
    return pl.pallas_call(
        _merge_kernel,
        grid=(m // tm, n // tn),
        in_specs=[pl.BlockSpec((tm, k), lambda i, j: (i, 0)),
                  pl.BlockSpec((tm, k), lambda i, j: (i, 0)),
                  pl.BlockSpec((k, tn), lambda i, j: (0, j)),
                  pl.BlockSpec((k, tn), lambda i, j: (0, j)),
                  pl.BlockSpec((tm, tn), lambda i, j: (i, ga_col + j)),
                  pl.BlockSpec((tm, tn), lambda i, j: (i, gb_col + j)),
                  pl.BlockSpec((2, tn), lambda i, j: (0, j))],
        out_specs=pl.BlockSpec((tm, tn), lambda i, j: (i, j)),
        out_shape=jax.ShapeDtypeStruct((m, n), BF16),
        compiler_params=_cparams("parallel", "parallel"),
    )(ya, yb, wa, wb, z, z, b_merge)


def _out_ln_kernel(m_ref, w_ref, x_ref, g_ref, b_ref, u_ref, ub_ref, *, alpha):
    mix = jnp.dot(m_ref[...], w_ref[...], preferred_element_type=F32)
    u = _layer_norm(alpha * x_ref[...] + mix, g_ref[...], b_ref[...])
    u_ref[...] = u
    ub_ref[...] = u.astype(BF16)


def out_proj_ln(merged, w_out, x, g, b, alpha, tm):
    m, k = merged.shape
    n = w_out.shape[1]
    row = lambda i: (i, 0)
    fixed = lambda i: (0, 0)
    return pl.pallas_call(
        functools.partial(_out_ln_kernel, alpha=alpha),
        grid=(m // tm,),
        in_specs=[pl.BlockSpec((tm, k), row), pl.BlockSpec((k, n), fixed), pl.BlockSpec((tm, n), row),
                  pl.BlockSpec((1, n), fixed), pl.BlockSpec((1, n), fixed)],
        out_specs=[pl.BlockSpec((tm, n), row), pl.BlockSpec((tm, n), row)],
        out_shape=[jax.ShapeDtypeStruct((m, n), F32), jax.ShapeDtypeStruct((m, n), BF16)],
        compiler_params=_cparams("parallel"),
    )(merged, w_out, x, g, b)


def _ple_kernel(u_ref, wg_ref, pe_ref, wp_ref, o_ref):
    gate = _sigmoid(jnp.dot(u_ref[...], wg_ref[...], preferred_element_type=F32))
    proj = jnp.dot(pe_ref[...], wp_ref[...], preferred_element_type=F32)
    o_ref[...] = (gate * proj).astype(o_ref.dtype)


def ple_term(ub, w_gate, pe, w_proj, tm, tn):
    m, k = ub.shape
    n = w_gate.shape[1]
    kp = pe.shape[1]
    return pl.pallas_call(
        _ple_kernel,
        grid=(m // tm, n // tn),
        in_specs=[pl.BlockSpec((tm, k), lambda i, j: (i, 0)),
                  pl.BlockSpec((k, tn), lambda i, j: (0, j)),
                  pl.BlockSpec((tm, kp), lambda i, j: (i, 0)),
                  pl.BlockSpec((kp, tn), lambda i, j: (0, j))],
        out_specs=pl.BlockSpec((tm, tn), lambda i, j: (i, j)),
        out_shape=jax.ShapeDtypeStruct((m, n), BF16),
        compiler_params=_cparams("parallel", "parallel"),
    )(ub, w_gate, pe, w_proj)


def _router_kernel(u_ref, wrt_ref, br_ref, comb_ref, sel_ref):
    scores = _sigmoid(_dot_nt(wrt_ref[...], u_ref[...], precision=HI))
    sel = scores + br_ref[...]
    n_exp, tm = sel.shape
    gsize = n_exp // N_GROUPS
    groups = [sel[g * gsize:(g + 1) * gsize] for g in range(N_GROUPS)]
    rows = []
    for x in groups:
        m1 = jnp.max(x, axis=0, keepdims=True)
        dup = jnp.sum(jnp.where(x == m1, 1.0, 0.0), axis=0, keepdims=True)
        m2 = jnp.max(jnp.where(x < m1, x, -jnp.inf), axis=0, keepdims=True)
        rows.append(m1 + jnp.where(dup >= 2.0, m1, m2))
    gscore = jnp.concatenate(rows, axis=0)
    gidx = lax.broadcasted_iota(jnp.int32, gscore.shape, 0)
    grank = jnp.zeros(gscore.shape, F32)
    for g in range(N_GROUPS):
        o = gscore[g:g + 1]
        grank = grank + jnp.where(o > gscore, 1.0, jnp.where(o == gscore, jnp.where(g < gidx, 1.0, 0.0), 0.0))
    masked = jnp.concatenate(
        [jnp.where(grank[g:g + 1] < TOPK_GROUPS, groups[g], NEG_INF) for g in range(N_GROUPS)], axis=0)
    sel_ref[...] = masked
    eidx = lax.broadcasted_iota(jnp.int32, masked.shape, 0)

    def count_better(e, rank):
        o = sel_ref[pl.ds(e, 1), :]
        return rank + jnp.where(o > masked, 1.0, jnp.where(o == masked, jnp.where(e < eidx, 1.0, 0.0), 0.0))

    rank = lax.fori_loop(0, n_exp, count_better, jnp.zeros(masked.shape, F32))
    w = jnp.where(rank < TOP_K, scores, 0.0)
    w = w / jnp.sum(w, axis=0, keepdims=True) * ROUTE_SCALE
    pad = jnp.zeros((comb_ref.shape[1] - n_exp, tm), F32)
    comb_ref[...] = jnp.concatenate([w, pad], axis=0).T


def router(u, w_router_t, b_router, tm):
    m, k = u.shape
    n_exp = w_router_t.shape[0]
    return pl.pallas_call(
        _router_kernel,
        grid=(m // tm,),
        in_specs=[pl.BlockSpec((tm, k), lambda i: (i, 0)),
                  pl.BlockSpec((n_exp, k), lambda i: (0, 0)),
                  pl.BlockSpec((n_exp, 1), lambda i: (0, 0))],
        out_specs=pl.BlockSpec((tm, LANES), lambda i: (i, 0)),
        out_shape=jax.ShapeDtypeStruct((m, LANES), F32),
        scratch_shapes=[pltpu.VMEM((n_exp, tm), F32)],
        compiler_params=_cparams("parallel"),
    )(u, w_router_t, b_router)


def _moe_kernel(u_ref, comb_ref, wg_ref, wu_ref, wd_ref, o_ref):
    e = pl.program_id(1)

    @pl.when(e == 0)
    def _():
        o_ref[...] = jnp.zeros_like(o_ref)

    u = u_ref[...]
    hg = jnp.dot(u, wg_ref[0].astype(BF16), preferred_element_type=F32)
    hu = jnp.dot(u, wu_ref[0].astype(BF16), preferred_element_type=F32)
    comb = comb_ref[...]
    lane = lax.broadcasted_iota(jnp.int32, comb.shape, 1)
    c = jnp.sum(jnp.where(lane == e, comb, 0.0), axis=1, keepdims=True)
    h = (_silu(hg) * hu * c).astype(BF16)
    o_ref[...] += jnp.dot(h, wd_ref[0].astype(BF16), preferred_element_type=F32)


def moe_routed(ub, comb, w_gate, w_up, w_down, tm):
    m, k = ub.shape
    n_exp, _, f = w_gate.shape
    return pl.pallas_call(
        _moe_kernel,
        grid=(m // tm, n_exp),
        in_specs=[pl.BlockSpec((tm, k), lambda i, e: (i, 0)),
                  pl.BlockSpec((tm, LANES), lambda i, e: (i, 0)),
                  pl.BlockSpec((1, k, f), lambda i, e: (e, 0, 0)),
                  pl.BlockSpec((1, k, f), lambda i, e: (e, 0, 0)),
                  pl.BlockSpec((1, f, k), lambda i, e: (e, 0, 0))],
        out_specs=pl.BlockSpec((tm, k), lambda i, e: (i, 0)),
        out_shape=jax.ShapeDtypeStruct((m, k), F32),
        compiler_params=_cparams("parallel", "arbitrary"),
    )(ub, comb, w_gate, w_up, w_down)


def _final_kernel(u_ref, ub_ref, routed_ref, ple_ref, wsg_ref, wsu_ref, wsd_ref, g_ref, b_ref, y_ref, *, alpha):
    ub = ub_ref[...]
    hs = _silu(jnp.dot(ub, wsg_ref[...], preferred_element_type=F32)) * jnp.dot(
        ub, wsu_ref[...], preferred_element_type=F32)
    shared = jnp.dot(hs.astype(BF16), wsd_ref[...], preferred_element_type=F32)
    h = alpha * u_ref[...] + (routed_ref[...] + shared) + ple_ref[...].astype(F32)
    y_ref[...] = _layer_norm(h, g_ref[...], b_ref[...])


def final_ln(u, ub, routed, ple, wsg, wsu, wsd, g, b, alpha, tm):
    m, n = u.shape
    f = wsg.shape[1]
    row = lambda i: (i, 0)
    fixed = lambda i: (0, 0)
    return pl.pallas_call(
        functools.partial(_final_kernel, alpha=alpha),
        grid=(m // tm,),
        in_specs=[pl.BlockSpec((tm, n), row), pl.BlockSpec((tm, n), row), pl.BlockSpec((tm, n), row),
                  pl.BlockSpec((tm, n), row), pl.BlockSpec((n, f), fixed), pl.BlockSpec((n, f), fixed),
                  pl.BlockSpec((f, n), fixed), pl.BlockSpec((1, n), fixed), pl.BlockSpec((1, n), fixed)],
        out_specs=pl.BlockSpec((tm, n), row),
        out_shape=jax.ShapeDtypeStruct((m, n), F32),
        compiler_params=_cparams("parallel"),
    )(u, ub, routed, ple, wsg, wsu, wsd, g, b)


TM_MATMUL = 832
TM_NORM = 416
TM_ROUTER = 640
TN_IN = 1024
TN_MERGE = 512
TN_PLE = 1024
HGRN_ROWS = 128
FOX_GATE_ROWS = 256
FOX_BLK = 512
FOX_SLOTS = 4
FOX_BIAS_PAGES = 64


def kernel(x_prompt, x_sample, p_prompt, p_sample, cache_k, cache_v, cache_lf, state_hgrn, page_table, w_in, b_fox_f, b_merge, lb_param, hgrn_norm_g, w_proj_a, w_proj_b, w_out, ln1_g, ln1_b, ln2_g, ln2_b, w_router, b_router, w_exp_gate, w_exp_up, w_exp_down, w_sh_gate, w_sh_up, w_sh_down, w_ple_proj, w_ple_gate):
    depth, d_model, _ = w_in.shape
    assert depth == 1 and x_prompt.shape[0] == 1, "single layer, single prompt sequence"
    t_p = x_prompt.shape[1]
    bsz, t_new, _ = x_sample.shape
    t_s = bsz * t_new
    n_ha, dk_a, dv_a = state_hgrn.shape[2:]
    _, pool, page, n_hb, dh_b = cache_k.shape
    assert dk_a == LANES and dv_a == LANES and dh_b == LANES and n_hb % 2 == 0
    wa_cols = n_ha * LANES
    wb_cols = n_hb * LANES
    c_qb = 4 * wa_cols
    c_fb = c_qb + 3 * wb_cols
    alpha = (2.0 * depth) ** 0.25
    l = 0

    x_all = jnp.concatenate([x_prompt.reshape(t_p, d_model), x_sample.reshape(t_s, d_model)], axis=0)
    xb = x_all.astype(BF16)
    w_l = w_in[l]
    w_cat = jnp.concatenate([w_l[:, :c_fb], w_l[:, c_fb + n_hb:]], axis=1).astype(BF16)
    w_fb = jnp.pad(w_l[:, c_fb:c_fb + n_hb], ((0, 0), (0, LANES - n_hb))).astype(BF16)
    z = matmul(xb, w_cat, TM_MATMUL, TN_IN)
    zf = matmul(xb, w_fb, TM_MATMUL, LANES)

    ng = hgrn_norm_g[l][None]
    ya_p, s_p = hgrn_prompt(z, lb_param, ng, t_p, n_ha, HGRN_ROWS)
    zs = jnp.pad(z[t_p:, :c_qb].reshape(bsz, t_new, c_qb), ((0, 0), (0, HGRN_BLK - t_new), (0, 0)))
    ya_s, s_s = hgrn_sample(zs.reshape(bsz * HGRN_BLK, c_qb), lb_param, ng, state_hgrn[l], n_ha, t_new)
    ya_s = ya_s.reshape(bsz, HGRN_BLK, wa_cols)[:, :t_new].reshape(t_s, wa_cols)

    fbias = jnp.pad(b_fox_f[l], (0, LANES - n_hb))[None]
    lf_p, ct = fox_gate_prompt(zf, fbias, t_p, FOX_GATE_ROWS)
    lf_s, c_s = fox_gate_sample(zf[t_p:], fbias, t_new)
    q_col = c_qb // LANES
    yb_p = fox_prompt(z, ct[:n_hb].reshape(n_hb, 1, t_p), t_p, n_hb,
                      q_col, q_col + n_hb, q_col + 2 * n_hb, FOX_BLK)
    z_s = z[t_p:]
    k_s = z_s[:, c_qb + wb_cols:c_qb + 2 * wb_cols]
    v_s = z_s[:, c_qb + 2 * wb_cols:c_fb]
    yb_s = fox_sample_wrapper(page_table, z_s[:, c_qb:c_qb + wb_cols], k_s, v_s, c_s,
                              cache_k, cache_v, cache_lf, l, bsz, t_new, FOX_SLOTS, FOX_BIAS_PAGES)

    ya = jnp.concatenate([ya_p, ya_s], axis=0)
    yb = jnp.concatenate([yb_p, yb_s.astype(BF16)], axis=0)
    merged = merge_branches(ya, yb, w_proj_a[l].astype(BF16), w_proj_b[l].astype(BF16), z,
                            c_fb // TN_MERGE, (c_fb + d_model) // TN_MERGE, b_merge[l], TM_MATMUL, TN_MERGE)
    u, ub = out_proj_ln(merged, w_out[l].astype(BF16), x_all, ln1_g[l][None], ln1_b[l][None], alpha, TM_NORM)

    pe = jnp.concatenate([p_prompt[l].reshape(t_p, -1), p_sample[l].reshape(t_s, -1)], axis=0).astype(BF16)
    ple = ple_term(ub, w_ple_gate[l].astype(BF16), pe, w_ple_proj[l].astype(BF16), TM_MATMUL, TN_PLE)
    comb = router(u, w_router[l].T, b_router[l][:, None], TM_ROUTER)
    routed = moe_routed(ub, comb, w_exp_gate[l], w_exp_up[l], w_exp_down[l], TM_MATMUL)
    y = final_ln(u, ub, routed, ple, w_sh_gate[l].astype(BF16), w_sh_up[l].astype(BF16),
                 w_sh_down[l].astype(BF16), ln2_g[l][None], ln2_b[l][None], alpha, TM_NORM)

    k_p = z[:t_p, c_qb + wb_cols:c_qb + 2 * wb_cols]
    v_p = z[:t_p, c_qb + 2 * wb_cols:c_fb]
    return (y[:t_p].reshape(1, t_p, d_model),
            y[t_p:].reshape(bsz, t_new, d_model),
            k_p.reshape(1, 1, t_p, n_hb, dh_b),
            v_p.reshape(1, 1, t_p, n_hb, dh_b),
            lf_p[:, :n_hb].reshape(1, 1, t_p, n_hb),
            s_p[None, None],
            k_s.reshape(1, bsz, t_new, n_hb, dh_b),
            v_s.reshape(1, bsz, t_new, n_hb, dh_b),
            lf_s[:, :n_hb].reshape(1, bsz, t_new, n_hb),
            s_s[None])
```

```python
import functools

import jax
import jax.numpy as jnp
from jax import lax
from jax.experimental import pallas as pl
from jax.experimental.pallas import tpu as pltpu

F32 = jnp.float32
BF16 = jnp.bfloat16

LANES = 128
VMEM_LIMIT = 56 * 1024 * 1024

LN_EPS = 1e-5
NEG_INF = -1e30
ROUTE_SCALE = 2.5
N_GROUPS = 8
TOPK_GROUPS = 4
TOP_K = 8
HGRN_BLK = 16

HI = lax.Precision.HIGHEST


def _cparams(*sem):
    return pltpu.CompilerParams(dimension_semantics=sem, vmem_limit_bytes=VMEM_LIMIT)


def _sigmoid(x):
    return 1.0 / (1.0 + jnp.exp(-x))


def _silu(x):
    return x * _sigmoid(x)


def _log_sigmoid(x):
    return jnp.minimum(x, 0.0) - jnp.log1p(jnp.exp(-jnp.abs(x)))


def _dot_nt(a, b, precision=None):
    return lax.dot_general(a, b, (((1,), (1,)), ((), ())), preferred_element_type=F32, precision=precision)


def _dot_tn(a, b, precision=None):
    return lax.dot_general(a, b, (((0,), (0,)), ((), ())), preferred_element_type=F32, precision=precision)


def _split3(x):
    hi = x.astype(BF16)
    rest = x - hi.astype(F32)
    mid = rest.astype(BF16)
    return hi, mid, (rest - mid.astype(F32)).astype(BF16)


def _dot_mask(mask, x):
    mb = mask.astype(BF16)
    hi, mid, lo = _split3(x)
    return (jnp.dot(mb, hi, preferred_element_type=F32) + jnp.dot(mb, mid, preferred_element_type=F32)
            + jnp.dot(mb, lo, preferred_element_type=F32))


def _dot_tn_mask(x, mask):
    mb = mask.astype(BF16)
    hi, mid, lo = _split3(x)
    return _dot_tn(hi, mb) + _dot_tn(mid, mb) + _dot_tn(lo, mb)


def _mm_kernel(x_ref, w_ref, o_ref):
    o_ref[...] = jnp.dot(x_ref[...], w_ref[...], preferred_element_type=F32).astype(o_ref.dtype)


def matmul(x, w, tm, tn, out_dtype=F32):
    m, k = x.shape
    n = w.shape[1]
    return pl.pallas_call(
        _mm_kernel,
        grid=(m // tm, n // tn),
        in_specs=[pl.BlockSpec((tm, k), lambda i, j: (i, 0)),
                  pl.BlockSpec((k, tn), lambda i, j: (0, j))],
        out_specs=pl.BlockSpec((tm, tn), lambda i, j: (i, j)),
        out_shape=jax.ShapeDtypeStruct((m, n), out_dtype),
        compiler_params=_cparams("parallel", "parallel"),
    )(x, w)


def _in_proj_kernel(x_ref, w_ref, o_ref, wb_ref):
    @pl.when(pl.program_id(1) == 0)
    def _():
        wb_ref[...] = w_ref[0].astype(BF16)

    o_ref[...] = jnp.dot(x_ref[...], wb_ref[...], preferred_element_type=F32)


def in_proj(xb, w, layer, n_cols, tm, tn):
    m, k = xb.shape
    return pl.pallas_call(
        _in_proj_kernel,
        grid=(n_cols // tn, m // tm),
        in_specs=[pl.BlockSpec((tm, k), lambda j, i: (i, 0)),
                  pl.BlockSpec((1, k, tn), lambda j, i: (layer, 0, j))],
        out_specs=pl.BlockSpec((tm, tn), lambda j, i: (i, j)),
        out_shape=jax.ShapeDtypeStruct((m, n_cols), F32),
        scratch_shapes=[pltpu.VMEM((k, tn), BF16)],
        compiler_params=_cparams("arbitrary", "arbitrary"),
    )(xb, w)


def _hgrn_head(zq, zf, zi, zg, lb, ng, st, mats, valid, dk):
    rows = zq.shape[0]
    q = _silu(zq) * dk ** -0.5
    logf = jnp.log(lb + (1.0 - lb) * _sigmoid(zf))
    k = (1.0 - lb) * _sigmoid(-zf)
    if valid is not None:
        logf = jnp.where(valid, logf, 0.0)
        k = jnp.where(valid, k, 0.0)
    both = jnp.dot(mats, logf, preferred_element_type=F32, precision=HI)
    b, blast = both[:rows], both[rows:]
    qt = (q * jnp.exp(b)).astype(BF16)
    kt = (k * jnp.exp(blast - b)).astype(BF16)
    vb = zi.astype(BF16)
    trow = lax.broadcasted_iota(jnp.int32, (HGRN_BLK, 1), 0)
    outs = []
    for i in range(rows // HGRN_BLK):
        sl = slice(i * HGRN_BLK, (i + 1) * HGRN_BLK)
        o_blk = _dot_nt(qt[sl], st.astype(BF16))
        qb, kb, bb, vv = q[sl], k[sl], b[sl], zi[sl]
        for s in range(HGRN_BLK):
            e = jnp.exp(jnp.minimum(bb - bb[s:s + 1], 0.0))
            col = jnp.sum(qb * kb[s:s + 1] * e, axis=1, keepdims=True)
            o_blk = o_blk + jnp.where(trow >= s, col, 0.0) * vv[s:s + 1]
        outs.append(o_blk)
        decay = jnp.exp(blast[i * HGRN_BLK:i * HGRN_BLK + 1])
        st = st * decay + _dot_tn(vb[sl], kt[sl])
    o = jnp.concatenate(outs, axis=0)
    o = o * lax.rsqrt(jnp.mean(jnp.square(o), axis=1, keepdims=True) + LN_EPS)
    return o * ng * _silu(zg), st


def _block_mats(rows):
    r = lax.broadcasted_iota(jnp.int32, (rows, rows), 0)
    c = lax.broadcasted_iota(jnp.int32, (rows, rows), 1)
    same = (r // HGRN_BLK) == (c // HGRN_BLK)
    lmat = jnp.where(same & (c <= r), 1.0, 0.0).astype(F32)
    bmat = jnp.where(same, 1.0, 0.0).astype(F32)
    return jnp.concatenate([lmat, bmat], axis=0)


def _lower_bound(lbp_ref, sl):
    p = lbp_ref[:, sl]
    m = jnp.max(p, axis=0, keepdims=True)
    e = jnp.exp(p - m)
    return e[0:1] / jnp.sum(e, axis=0, keepdims=True)


def _hgrn_prompt_kernel(zq_ref, zf_ref, zi_ref, zg_ref, lbp_ref, ng_ref, y_ref, s_ref, st_ref, *, n_heads, dk):
    step = pl.program_id(0)

    @pl.when(step == 0)
    def _():
        st_ref[...] = jnp.zeros_like(st_ref)

    mats = _block_mats(zq_ref.shape[0])

    def head(h, carry):
        sl = pl.ds(pl.multiple_of(h * LANES, LANES), LANES)
        y, st = _hgrn_head(zq_ref[:, sl], zf_ref[:, sl], zi_ref[:, sl], zg_ref[:, sl],
                           _lower_bound(lbp_ref, sl), ng_ref[:, sl], st_ref[h], mats, None, dk)
        y_ref[:, sl] = y.astype(y_ref.dtype)
        st_ref[h] = st
        return carry

    lax.fori_loop(0, n_heads, head, 0, unroll=2)

    @pl.when(step == pl.num_programs(0) - 1)
    def _():
        for h in range(n_heads):
            s_ref[h] = st_ref[h].T


def hgrn_prompt(z, lb_param, norm_g, t_len, n_heads, rows):
    width = n_heads * LANES
    kern = functools.partial(_hgrn_prompt_kernel, n_heads=n_heads, dk=LANES)
    return pl.pallas_call(
        kern,
        grid=(t_len // rows,),
        in_specs=[pl.BlockSpec((rows, width), lambda i: (i, 0)),
                  pl.BlockSpec((rows, width), lambda i: (i, 1)),
                  pl.BlockSpec((rows, width), lambda i: (i, 2)),
                  pl.BlockSpec((rows, width), lambda i: (i, 3)),
                  pl.BlockSpec((2, width), lambda i: (0, 0)),
                  pl.BlockSpec((1, width), lambda i: (0, 0))],
        out_specs=[pl.BlockSpec((rows, width), lambda i: (i, 0)),
                   pl.BlockSpec((n_heads, LANES, LANES), lambda i: (0, 0, 0))],
        out_shape=[jax.ShapeDtypeStruct((t_len, width), BF16),
                   jax.ShapeDtypeStruct((n_heads, LANES, LANES), F32)],
        scratch_shapes=[pltpu.VMEM((n_heads, LANES, LANES), F32)],
        compiler_params=_cparams("arbitrary"),
    )(z, z, z, z, lb_param, norm_g)


def _hgrn_sample_kernel(zq_ref, zf_ref, zi_ref, zg_ref, lbp_ref, ng_ref, s0_ref, y_ref, s_ref, *, n_heads, dk, t_new):
    rows = zq_ref.shape[0]
    mats = _block_mats(rows)
    valid = lax.broadcasted_iota(jnp.int32, (rows, 1), 0) < t_new

    def head(h, carry):
        sl = pl.ds(pl.multiple_of(h * LANES, LANES), LANES)
        y, st = _hgrn_head(zq_ref[:, sl], zf_ref[:, sl], zi_ref[:, sl], zg_ref[:, sl],
                           _lower_bound(lbp_ref, sl), ng_ref[:, sl], s0_ref[0, h].T, mats, valid, dk)
        y_ref[:, sl] = y.astype(y_ref.dtype)
        s_ref[0, h] = st.T
        return carry

    lax.fori_loop(0, n_heads, head, 0, unroll=2)


def hgrn_sample(zs, lb_param, norm_g, s0, n_heads, t_new):
    bsz = s0.shape[0]
    width = n_heads * LANES
    kern = functools.partial(_hgrn_sample_kernel, n_heads=n_heads, dk=LANES, t_new=t_new)
    return pl.pallas_call(
        kern,
        grid=(bsz,),
        in_specs=[pl.BlockSpec((HGRN_BLK, width), lambda b: (b, 0)),
                  pl.BlockSpec((HGRN_BLK, width), lambda b: (b, 1)),
                  pl.BlockSpec((HGRN_BLK, width), lambda b: (b, 2)),
                  pl.BlockSpec((HGRN_BLK, width), lambda b: (b, 3)),
                  pl.BlockSpec((2, width), lambda b: (0, 0)),
                  pl.BlockSpec((1, width), lambda b: (0, 0)),
                  pl.BlockSpec((1, n_heads, LANES, LANES), lambda b: (b, 0, 0, 0))],
        out_specs=[pl.BlockSpec((HGRN_BLK, width), lambda b: (b, 0)),
                   pl.BlockSpec((1, n_heads, LANES, LANES), lambda b: (b, 0, 0, 0))],
        out_shape=[jax.ShapeDtypeStruct((bsz * HGRN_BLK, width), BF16),
                   jax.ShapeDtypeStruct(s0.shape, F32)],
        compiler_params=_cparams("parallel"),
    )(zs, zs, zs, zs, lb_param, norm_g, s0)


def _fox_gate_prompt_kernel(zf_ref, bias_ref, lf_ref, ct_ref, carry_ref):
    @pl.when(pl.program_id(0) == 0)
    def _():
        carry_ref[...] = jnp.zeros_like(carry_ref)

    lf = _log_sigmoid(zf_ref[...] + bias_ref[...])
    lf_ref[...] = lf
    rows = lf.shape[0]
    r = lax.broadcasted_iota(jnp.int32, (rows, rows), 0)
    c = lax.broadcasted_iota(jnp.int32, (rows, rows), 1)
    lower = jnp.where(c <= r, 1.0, 0.0).astype(BF16)
    cs = _dot_mask(lower, lf) + carry_ref[...]
    ct_ref[...] = cs
    carry_ref[...] = cs[rows - 1:rows]


def fox_gate_prompt(zf, bias, t_len, rows):
    return pl.pallas_call(
        _fox_gate_prompt_kernel,
        grid=(t_len // rows,),
        in_specs=[pl.BlockSpec((rows, LANES), lambda i: (i, 0)),
                  pl.BlockSpec((1, LANES), lambda i: (0, 0))],
        out_specs=[pl.BlockSpec((rows, LANES), lambda i: (i, 0)),
                   pl.BlockSpec((rows, LANES), lambda i: (i, 0))],
        out_shape=[jax.ShapeDtypeStruct((t_len, LANES), F32),
                   jax.ShapeDtypeStruct((t_len, LANES), F32)],
        scratch_shapes=[pltpu.VMEM((1, LANES), F32)],
        compiler_params=_cparams("arbitrary"),
    )(zf, bias)


def _fox_gate_sample_kernel(zf_ref, bias_ref, lf_ref, c_ref, *, t_new):
    lf = _log_sigmoid(zf_ref[...] + bias_ref[...])
    lf_ref[...] = lf
    rows = lf.shape[0]
    r = lax.broadcasted_iota(jnp.int32, (rows, rows), 0)
    c = lax.broadcasted_iota(jnp.int32, (rows, rows), 1)
    tri = jnp.where(((r // t_new) == (c // t_new)) & (c <= r), 1.0, 0.0).astype(BF16)
    c_ref[...] = _dot_mask(tri, lf)


def fox_gate_sample(zf, bias, t_new):
    rows = zf.shape[0]
    return pl.pallas_call(
        functools.partial(_fox_gate_sample_kernel, t_new=t_new),
        grid=(1,),
        in_specs=[pl.BlockSpec((rows, LANES), lambda i: (0, 0)),
                  pl.BlockSpec((1, LANES), lambda i: (0, 0))],
        out_specs=[pl.BlockSpec((rows, LANES), lambda i: (0, 0)),
                   pl.BlockSpec((rows, LANES), lambda i: (0, 0))],
        out_shape=[jax.ShapeDtypeStruct((rows, LANES), F32),
                   jax.ShapeDtypeStruct((rows, LANES), F32)],
        compiler_params=_cparams("arbitrary"),
    )(zf, bias)


LOG2E = 1.4426950408889634
FOX_ROW_CHUNKS = 4


def _fox_prompt_kernel(q_ref, k_ref, v_ref, c_ref, o_ref, kb_ref, vb_ref, *, blk, scale):
    h = pl.program_id(0)
    i = pl.program_id(1)
    t_len = k_ref.shape[0]

    @pl.when(i == 0)
    def _():
        lane = lax.broadcasted_iota(jnp.int32, (t_len, LANES), 1)
        beta = -LOG2E * jnp.sum(jnp.where(lane == h, c_ref[...], 0.0), axis=1, keepdims=True)
        hi = beta.astype(BF16).astype(F32)
        mid = (beta - hi).astype(BF16).astype(F32)
        lo = beta - hi - mid
        aug = jnp.where(lane == 0, hi, jnp.where(lane == 1, mid, jnp.where(lane == 2, lo, 0.0)))
        kb_ref[:, :LANES] = k_ref[...].astype(BF16)
        kb_ref[:, LANES:] = aug.astype(BF16)
        vb_ref[:, :LANES] = v_ref[...].astype(BF16)
        vb_ref[:, LANES:] = jnp.where(lane == 0, 1.0, 0.0).astype(BF16)

    qlane = lax.broadcasted_iota(jnp.int32, (blk, LANES), 1)
    q = jnp.concatenate([(q_ref[...] * (scale * LOG2E)).astype(BF16),
                         jnp.where(qlane < 3, 1.0, 0.0).astype(BF16)], axis=1)

    sub = blk // FOX_ROW_CHUNKS
    qs = [q[r * sub:(r + 1) * sub] for r in range(FOX_ROW_CHUNKS)]
    row = lax.broadcasted_iota(jnp.int32, (sub, blk), 0)
    col = lax.broadcasted_iota(jnp.int32, (sub, blk), 1)

    def update(j, carry, diagonal):
        off = pl.multiple_of(j * blk, blk)
        kj = kb_ref[pl.ds(off, blk), :]
        vj = vb_ref[pl.ds(off, blk), :]
        out = []
        for r, (m, acc) in enumerate(carry):
            s = _dot_nt(qs[r], kj)
            if diagonal:
                s = jnp.where(col <= row + r * sub, s, NEG_INF)
            m_new = jnp.maximum(m, jnp.max(s, axis=1, keepdims=True))
            p = jnp.exp2(s - m_new).astype(BF16)
            out.append((m_new, jnp.exp2(m - m_new) * acc + jnp.dot(p, vj, preferred_element_type=F32)))
        return tuple(out)

    init = tuple((jnp.full((sub, 1), NEG_INF, F32), jnp.zeros((sub, 2 * LANES), F32))
                 for _ in range(FOX_ROW_CHUNKS))
    carry = lax.fori_loop(0, i, lambda j, c: update(j, c, False), init)
    carry = update(i, carry, True)
    for r, (_, acc) in enumerate(carry):
        o_ref[r * sub:(r + 1) * sub, :] = (acc[:, :LANES] / acc[:, LANES:LANES + 1]).astype(o_ref.dtype)


def fox_prompt(z, c, t_len, n_heads, q_col, k_col, v_col, blk):
    kern = functools.partial(_fox_prompt_kernel, blk=blk, scale=LANES ** -0.5)
    return pl.pallas_call(
        kern,
        grid=(n_heads, t_len // blk),
        in_specs=[pl.BlockSpec((blk, LANES), lambda h, i: (i, q_col + h)),
                  pl.BlockSpec((t_len, LANES), lambda h, i: (0, k_col + h)),
                  pl.BlockSpec((t_len, LANES), lambda h, i: (0, v_col + h)),
                  pl.BlockSpec((t_len, LANES), lambda h, i: (0, 0))],
        out_specs=pl.BlockSpec((blk, LANES), lambda h, i: (i, h)),
        out_shape=jax.ShapeDtypeStruct((t_len, n_heads * LANES), BF16),
        scratch_shapes=[pltpu.VMEM((t_len, 2 * LANES), BF16), pltpu.VMEM((t_len, 2 * LANES), BF16)],
        compiler_params=_cparams("arbitrary", "arbitrary"),
    )(z, z, z, c)


HEAD_GROUP = 8


def _cache_bias_kernel(lf_ref, o_ref):
    page = lf_ref.shape[2]
    ks = lax.broadcasted_iota(jnp.int32, (page, 2 * page), 0)
    kj = lax.broadcasted_iota(jnp.int32, (page, 2 * page), 1)
    later = jnp.where(ks > kj, 1.0, jnp.where(kj >= page, 1.0, 0.0)).astype(BF16)

    def one(p, carry):
        o_ref[p] = _dot_tn_mask(lf_ref[0, p], later)
        return carry

    lax.fori_loop(0, lf_ref.shape[1], one, 0, unroll=8)


def fox_cache_bias(cache_lf, layer, pages_per_step):
    depth, pool, page, n_heads = cache_lf.shape
    return pl.pallas_call(
        _cache_bias_kernel,
        grid=(pool // pages_per_step,),
        in_specs=[pl.BlockSpec((1, pages_per_step, page, n_heads), lambda i: (layer, i, 0, 0))],
        out_specs=pl.BlockSpec((pages_per_step, n_heads, 2 * page), lambda i: (i, 0, 0)),
        out_shape=jax.ShapeDtypeStruct((pool, n_heads, 2 * page), F32),
        compiler_params=_cparams("parallel"),
    )(cache_lf)


def _fox_sample_kernel(pt_ref, q_ref, cnk_ref, kn_ref, vn_ref, *rest, n_slots, t_new):
    k_refs, v_refs, b_refs = rest[:n_slots], rest[n_slots:2 * n_slots], rest[2 * n_slots:3 * n_slots]
    o_ref = rest[3 * n_slots]
    m_ref, l_ref, acc_ref, carry_ref = rest[3 * n_slots + 1:]
    step = pl.program_id(1)
    n_groups = q_ref.shape[1]
    grows = q_ref.shape[2]
    page = k_refs[0].shape[2]

    def own_head(n_keys):
        r = lax.broadcasted_iota(jnp.int32, (grows, n_keys * HEAD_GROUP), 0)
        c = lax.broadcasted_iota(jnp.int32, (grows, n_keys * HEAD_GROUP), 1)
        return r, c, (c % HEAD_GROUP) == (r // t_new)

    def tile(ref, lead, g):
        x = ref[lead + (slice(None), slice(g * HEAD_GROUP, (g + 1) * HEAD_GROUP), slice(None))]
        return x.reshape(x.shape[0] * HEAD_GROUP, x.shape[2]).astype(BF16)

    @pl.when(step == 0)
    def _():
        r, c, same = own_head(2 * t_new)
        visible = same & ((c // HEAD_GROUP) <= (r % t_new))
        for g in range(n_groups):
            s = _dot_nt(q_ref[0, g].astype(BF16), tile(kn_ref, (0,), g))
            s = jnp.where(visible, s - cnk_ref[0, g], NEG_INF)
            m0 = jnp.max(s, axis=1, keepdims=True)
            p0 = jnp.exp(s - m0)
            m_ref[g] = m0
            l_ref[g] = jnp.sum(p0, axis=1, keepdims=True)
            acc_ref[g] = jnp.dot(p0.astype(BF16), tile(vn_ref, (0,), g), preferred_element_type=F32)
        carry_ref[...] = jnp.zeros_like(carry_ref)

    _, _, same = own_head(page)
    other_head = jnp.where(same, 0.0, NEG_INF)
    for g in range(n_groups):
        qg = q_ref[0, g].astype(BF16)
        carry = carry_ref[g]
        parts = []
        for r in range(n_slots):
            bias = other_head + (carry + b_refs[r][0, g, 0:1, :])
            parts.append(_dot_nt(qg, tile(k_refs[r], (0, 0), g)) + bias)
            carry = carry + b_refs[r][0, g, 1:2, :]
        carry_ref[g] = carry
        m = m_ref[g]
        m_new = m
        for s in parts:
            m_new = jnp.maximum(m_new, jnp.max(s, axis=1, keepdims=True))
        alpha = jnp.exp(m - m_new)
        l = alpha * l_ref[g]
        acc = alpha * acc_ref[g]
        for r, s in enumerate(parts):
            p = jnp.exp(s - m_new)
            l = l + jnp.sum(p, axis=1, keepdims=True)
            acc = acc + jnp.dot(p.astype(BF16), tile(v_refs[r], (0, 0), g), preferred_element_type=F32)
        m_ref[g], l_ref[g], acc_ref[g] = m_new, l, acc

    @pl.when(step == pl.num_programs(1) - 1)
    def _():
        for g in range(n_groups):
            o_ref[0, g] = acc_ref[g] / l_ref[g]


def fox_sample(page_table, q, cnk, k_new, v_new, cache_k, cache_v, bias_rows, layer, n_slots, t_new):
    bsz, n_pages = page_table.shape
    _, n_groups, grows, _ = q.shape
    _, _, page, n_heads, dh = cache_k.shape
    cols = page * HEAD_GROUP

    def page_idx(b, g, pt, r):
        return pt[b, n_pages - 1 - (g * n_slots + r)]

    def per_batch(shape):
        return pl.BlockSpec((1,) + shape, lambda b, g, pt: (b,) + (0,) * len(shape))

    in_specs = [per_batch((n_groups, grows, dh)),
                per_batch((n_groups, 1, 2 * t_new * HEAD_GROUP)),
                per_batch((2 * t_new, n_heads, dh)), per_batch((2 * t_new, n_heads, dh))]
    for _ in range(2):
        in_specs += [pl.BlockSpec((1, 1, page, n_heads, dh),
                                  functools.partial(lambda b, g, pt, r: (layer, page_idx(b, g, pt, r), 0, 0, 0), r=r))
                     for r in range(n_slots)]
    in_specs += [pl.BlockSpec((1, n_groups, 2, cols),
                              functools.partial(lambda b, g, pt, r: (page_idx(b, g, pt, r), 0, 0, 0), r=r))
                 for r in range(n_slots)]
    grid_spec = pltpu.PrefetchScalarGridSpec(
        num_scalar_prefetch=1,
        grid=(bsz, n_pages // n_slots),
        in_specs=in_specs,
        out_specs=per_batch((n_groups, grows, dh)),
        scratch_shapes=[pltpu.VMEM((n_groups, grows, 1), F32), pltpu.VMEM((n_groups, grows, 1), F32),
                        pltpu.VMEM((n_groups, grows, dh), F32), pltpu.VMEM((n_groups, 1, cols), F32)],
    )
    kern = functools.partial(_fox_sample_kernel, n_slots=n_slots, t_new=t_new)
    return pl.pallas_call(
        kern,
        grid_spec=grid_spec,
        out_shape=jax.ShapeDtypeStruct(q.shape, F32),
        compiler_params=_cparams("arbitrary", "arbitrary"),
    )(page_table, q, cnk, k_new, v_new,
      *([cache_k] * n_slots), *([cache_v] * n_slots), *([bias_rows] * n_slots))


def fox_sample_wrapper(page_table, zq, zk, zv, c_new, cache_k, cache_v, cache_lf, layer, bsz, t_new, n_slots,
                       bias_pages_per_step):
    _, pool, page, n_heads, dh = cache_k.shape
    n_groups = n_heads // HEAD_GROUP
    grows = HEAD_GROUP * t_new
    q = (zq * dh ** -0.5).reshape(bsz, t_new, n_heads, dh).transpose(0, 2, 1, 3).reshape(bsz, n_groups, grows, dh)
    c = c_new[:, :n_heads].reshape(bsz, t_new, n_groups, HEAD_GROUP)
    cnk = jnp.pad(c.transpose(0, 2, 1, 3), ((0, 0), (0, 0), (0, t_new), (0, 0)))
    cnk = cnk.reshape(bsz, n_groups, 1, 2 * t_new * HEAD_GROUP)
    pad = ((0, 0), (0, t_new), (0, 0), (0, 0))
    k_new = jnp.pad(zk.reshape(bsz, t_new, n_heads, dh), pad)
    v_new = jnp.pad(zv.reshape(bsz, t_new, n_heads, dh), pad)
    bias = fox_cache_bias(cache_lf, layer, bias_pages_per_step)
    bias_rows = bias.reshape(pool, n_groups, HEAD_GROUP, 2, page).transpose(0, 1, 3, 4, 2)
    bias_rows = bias_rows.reshape(pool, n_groups, 2, page * HEAD_GROUP)
    o = fox_sample(page_table, q, cnk, k_new, v_new, cache_k, cache_v, bias_rows, layer, n_slots, t_new)
    return o.reshape(bsz, n_heads, t_new, dh).transpose(0, 2, 1, 3).reshape(bsz * t_new, n_heads * dh)


def _layer_norm(h, g, b):
    mu = jnp.mean(h, axis=1, keepdims=True)
    d = h - mu
    var = jnp.mean(jnp.square(d), axis=1, keepdims=True)
    return d * lax.rsqrt(var + LN_EPS) * g + b


def _merge_kernel(ya_ref, yb_ref, wa_ref, wb_ref, ga_ref, gb_ref, bm_ref, o_ref):
    a = jnp.dot(ya_ref[...], wa_ref[...], preferred_element_type=F32)
    b = jnp.dot(yb_ref[...], wb_ref[...], preferred_element_type=F32)
    o = _sigmoid(ga_ref[...] + bm_ref[0:1]) * a + _sigmoid(gb_ref[...] + bm_ref[1:2]) * b
    o_ref[...] = o.astype(o_ref.dtype)


def merge_branches(ya, yb, wa, wb, z, ga_col, gb_col, b_merge, tm, tn):
    m, k = ya.shape
    n = wa.shape[1]
    return pl.pallas_call(
        _merge_kernel,
        grid=(m // tm, n // tn),
        in_specs=[pl.BlockSpec((tm, k), lambda i, j: (i, 0)),
                  pl.BlockSpec((tm, k), lambda i, j: (i, 0)),
                  pl.BlockSpec((k, tn), lambda i, j: (0, j)),
                  pl.BlockSpec((k, tn), lambda i, j: (0, j)),
                  pl.BlockSpec((tm, tn), lambda i, j: (i, ga_col + j)),
                  pl.BlockSpec((tm, tn), lambda i, j: (i, gb_col + j)),
                  pl.BlockSpec((2, tn), lambda i, j: (0, j))],
        out_specs=pl.BlockSpec((tm, tn), lambda i, j: (i, j)),
        out_shape=jax.ShapeDtypeStruct((m, n), BF16),
        compiler_params=_cparams("parallel", "parallel"),
    )(ya, yb, wa, wb, z, z, b_merge)


def _out_ln_kernel(m_ref, w_ref, x_ref, g_ref, b_ref, u_ref, ub_ref, *, alpha):
    mix = jnp.dot(m_ref[...], w_ref[...], preferred_element_type=F32)
    u = _layer_norm(alpha * x_ref[...] + mix, g_ref[...], b_ref[...])
    u_ref[...] = u
    ub_ref[...] = u.astype(BF16)


def out_proj_ln(merged, w_out, x, g, b, alpha, tm):
    m, k = merged.shape
    n = w_out.shape[1]
    row = lambda i: (i, 0)
    fixed = lambda i: (0, 0)
    return pl.pallas_call(
        functools.partial(_out_ln_kernel, alpha=alpha),
        grid=(m // tm,),
        in_specs=[pl.BlockSpec((tm, k), row), pl.BlockSpec((k, n), fixed), pl.BlockSpec((tm, n), row),
                  pl.BlockSpec((1, n), fixed), pl.BlockSpec((1, n), fixed)],
        out_specs=[pl.BlockSpec((tm, n), row), pl.BlockSpec((tm, n), row)],
        out_shape=[jax.ShapeDtypeStruct((m, n), F32), jax.ShapeDtypeStruct((m, n), BF16)],
        compiler_params=_cparams("parallel"),
    )(merged, w_out, x, g, b)


def _ple_kernel(u_ref, wg_ref, pe_ref, wp_ref, o_ref):
    gate = _sigmoid(jnp.dot(u_ref[...], wg_ref[...], preferred_element_type=F32))
    proj = jnp.dot(pe_ref[...], wp_ref[...], preferred_element_type=F32)
    o_ref[...] = (gate * proj).astype(o_ref.dtype)


def ple_term(ub, w_gate, pe, w_proj, tm, tn):
    m, k = ub.shape
    n = w_gate.shape[1]
    kp = pe.shape[1]
    return pl.pallas_call(
        _ple_kernel,
        grid=(m // tm, n // tn),
        in_specs=[pl.BlockSpec((tm, k), lambda i, j: (i, 0)),
                  pl.BlockSpec((k, tn), lambda i, j: (0, j)),
                  pl.BlockSpec((tm, kp), lambda i, j: (i, 0)),
                  pl.BlockSpec((kp, tn), lambda i, j: (0, j))],
        out_specs=pl.BlockSpec((tm, tn), lambda i, j: (i, j)),
        out_shape=jax.ShapeDtypeStruct((m, n), BF16),
        compiler_params=_cparams("parallel", "parallel"),
    )(ub, w_gate, pe, w_proj)


def _router_kernel(u_ref, wrt_ref, br_ref, comb_ref, sel_ref):
    scores = _sigmoid(_dot_nt(wrt_ref[...], u_ref[...], precision=HI))
    sel = scores + br_ref[...]
    n_exp, tm = sel.shape
    gsize = n_exp // N_GROUPS
    groups = [sel[g * gsize:(g + 1) * gsize] for g in range(N_GROUPS)]
    rows = []
    for x in groups:
        m1 = jnp.max(x, axis=0, keepdims=True)
        dup = jnp.sum(jnp.where(x == m1, 1.0, 0.0), axis=0, keepdims=True)
        m2 = jnp.max(jnp.where(x < m1, x, -jnp.inf), axis=0, keepdims=True)
        rows.append(m1 + jnp.where(dup >= 2.0, m1, m2))
    gscore = jnp.concatenate(rows, axis=0)
    gidx = lax.broadcasted_iota(jnp.int32, gscore.shape, 0)
    grank = jnp.zeros(gscore.shape, F32)
    for g in range(N_GROUPS):
        o = gscore[g:g + 1]
        grank = grank + jnp.where(o > gscore, 1.0, jnp.where(o == gscore, jnp.where(g < gidx, 1.0, 0.0), 0.0))
    masked = jnp.concatenate(
        [jnp.where(grank[g:g + 1] < TOPK_GROUPS, groups[g], NEG_INF) for g in range(N_GROUPS)], axis=0)
    sel_ref[...] = masked
    eidx = lax.broadcasted_iota(jnp.int32, masked.shape, 0)

    def count_better(e, rank):
        o = sel_ref[pl.ds(e, 1), :]
        return rank + jnp.where(o > masked, 1.0, jnp.where(o == masked, jnp.where(e < eidx, 1.0, 0.0), 0.0))

    rank = lax.fori_loop(0, n_exp, count_better, jnp.zeros(masked.shape, F32))
    w = jnp.where(rank < TOP_K, scores, 0.0)
    w = w / jnp.sum(w, axis=0, keepdims=True) * ROUTE_SCALE
    pad = jnp.zeros((comb_ref.shape[1] - n_exp, tm), F32)
    comb_ref[...] = jnp.concatenate([w, pad], axis=0).T


def router(u, w_router_t, b_router, tm):
    m, k = u.shape
    n_exp = w_router_t.shape[0]
    return pl.pallas_call(
        _router_kernel,
        grid=(m // tm,),
        in_specs=[pl.BlockSpec((tm, k), lambda i: (i, 0)),
                  pl.BlockSpec((n_exp, k), lambda i: (0, 0)),
                  pl.BlockSpec((n_exp, 1), lambda i: (0, 0))],
        out_specs=pl.BlockSpec((tm, LANES), lambda i: (i, 0)),
        out_shape=jax.ShapeDtypeStruct((m, LANES), F32),
        scratch_shapes=[pltpu.VMEM((n_exp, tm), F32)],
        compiler_params=_cparams("parallel"),
    )(u, w_router_t, b_router)


def _moe_kernel(u_ref, comb_ref, wg_ref, wu_ref, wd_ref, o_ref):
    e = pl.program_id(1)

    @pl.when(e == 0)
    def _():
        o_ref[...] = jnp.zeros_like(o_ref)

    u = u_ref[...]
    hg = jnp.dot(u, wg_ref[0].astype(BF16), preferred_element_type=F32)
    hu = jnp.dot(u, wu_ref[0].astype(BF16), preferred_element_type=F32)
    comb = comb_ref[...]
    lane = lax.broadcasted_iota(jnp.int32, comb.shape, 1)
    c = jnp.sum(jnp.where(lane == e, comb, 0.0), axis=1, keepdims=True)
    h = (_silu(hg) * hu * c).astype(BF16)
    o_ref[...] += jnp.dot(h, wd_ref[0].astype(BF16), preferred_element_type=F32)


def moe_routed(ub, comb, w_gate, w_up, w_down, tm):
    m, k = ub.shape
    n_exp, _, f = w_gate.shape
    return pl.pallas_call(
        _moe_kernel,
        grid=(m // tm, n_exp),
        in_specs=[pl.BlockSpec((tm, k), lambda i, e: (i, 0)),
                  pl.BlockSpec((tm, LANES), lambda i, e: (i, 0)),
                  pl.BlockSpec((1, k, f), lambda i, e: (e, 0, 0)),
                  pl.BlockSpec((1, k, f), lambda i, e: (e, 0, 0)),
                  pl.BlockSpec((1, f, k), lambda i, e: (e, 0, 0))],
        out_specs=pl.BlockSpec((tm, k), lambda i, e: (i, 0)),
        out_shape=jax.ShapeDtypeStruct((m, k), F32),
        compiler_params=_cparams("parallel", "arbitrary"),
    )(ub, comb, w_gate, w_up, w_down)


def _final_kernel(u_ref, ub_ref, routed_ref, ple_ref, wsg_ref, wsu_ref, wsd_ref, g_ref, b_ref, y_ref, *, alpha):
    ub = ub_ref[...]
    hs = _silu(jnp.dot(ub, wsg_ref[...], preferred_element_type=F32)) * jnp.dot(
        ub, wsu_ref[...], preferred_element_type=F32)
    shared = jnp.dot(hs.astype(BF16), wsd_ref[...], preferred_element_type=F32)
    h = alpha * u_ref[...] + (routed_ref[...] + shared) + ple_ref[...].astype(F32)
    y_ref[...] = _layer_norm(h, g_ref[...], b_ref[...])


def final_ln(u, ub, routed, ple, wsg, wsu, wsd, g, b, alpha, tm):
    m, n = u.shape
    f = wsg.shape[1]
    row = lambda i: (i, 0)
    fixed = lambda i: (0, 0)
    return pl.pallas_call(
        functools.partial(_final_kernel, alpha=alpha),
        grid=(m // tm,),
        in_specs=[pl.BlockSpec((tm, n), row), pl.BlockSpec((tm, n), row), pl.BlockSpec((tm, n), row),
                  pl.BlockSpec((tm, n), row), pl.BlockSpec((n, f), fixed), pl.BlockSpec((n, f), fixed),
                  pl.BlockSpec((f, n), fixed), pl.BlockSpec((1, n), fixed), pl.BlockSpec((1, n), fixed)],
        out_specs=pl.BlockSpec((tm, n), row),
        out_shape=jax.ShapeDtypeStruct((m, n), F32),
        compiler_params=_cparams("parallel"),
    )(u, ub, routed, ple, wsg, wsu, wsd, g, b)


TM_MATMUL = 832
TM_NORM = 416
TM_ROUTER = 640
TN_IN = 1024
TN_MERGE = 512
TN_PLE = 1024
HGRN_ROWS = 128
FOX_GATE_ROWS = 256
FOX_BLK = 1024
FOX_SLOTS = 8
FOX_BIAS_PAGES = 64


def kernel(x_prompt, x_sample, p_prompt, p_sample, cache_k, cache_v, cache_lf, state_hgrn, page_table, w_in, b_fox_f, b_merge, lb_param, hgrn_norm_g, w_proj_a, w_proj_b, w_out, ln1_g, ln1_b, ln2_g, ln2_b, w_router, b_router, w_exp_gate, w_exp_up, w_exp_down, w_sh_gate, w_sh_up, w_sh_down, w_ple_proj, w_ple_gate):
    depth, d_model, _ = w_in.shape
    assert depth == 1 and x_prompt.shape[0] == 1, "single layer, single prompt sequence"
    t_p = x_prompt.shape[1]
    bsz, t_new, _ = x_sample.shape
    t_s = bsz * t_new
    n_ha, dk_a, dv_a = state_hgrn.shape[2:]
    _, pool, page, n_hb, dh_b = cache_k.shape
    assert dk_a == LANES and dv_a == LANES and dh_b == LANES and n_hb % 2 == 0
    wa_cols = n_ha * LANES
    wb_cols = n_hb * LANES
    c_qb = 4 * wa_cols
    c_fb = c_qb + 3 * wb_cols
    alpha = (2.0 * depth) ** 0.25
    l = 0

    x_all = jnp.concatenate([x_prompt.reshape(t_p, d_model), x_sample.reshape(t_s, d_model)], axis=0)
    xb = x_all.astype(BF16)
    w_l = w_in[l]
    w_fb = jnp.pad(w_l[:, c_fb:c_fb + n_hb], ((0, 0), (0, LANES - n_hb))).astype(BF16)
    z = in_proj(xb, w_in, l, c_fb, TM_MATMUL, TN_IN)
    zg = matmul(xb, w_l[:, c_fb + n_hb:].astype(BF16), TM_MATMUL, TN_IN)
    zf = matmul(xb, w_fb, TM_MATMUL, LANES)

    ng = hgrn_norm_g[l][None]
    ya_p, s_p = hgrn_prompt(z, lb_param, ng, t_p, n_ha, HGRN_ROWS)
    zs = jnp.pad(z[t_p:, :c_qb].reshape(bsz, t_new, c_qb), ((0, 0), (0, HGRN_BLK - t_new), (0, 0)))
    ya_s, s_s = hgrn_sample(zs.reshape(bsz * HGRN_BLK, c_qb), lb_param, ng, state_hgrn[l], n_ha, t_new)
    ya_s = ya_s.reshape(bsz, HGRN_BLK, wa_cols)[:, :t_new].reshape(t_s, wa_cols)

    fbias = jnp.pad(b_fox_f[l], (0, LANES - n_hb))[None]
    lf_p, c_p = fox_gate_prompt(zf, fbias, t_p, FOX_GATE_ROWS)
    lf_s, c_s = fox_gate_sample(zf[t_p:], fbias, t_new)
    q_col = c_qb // LANES
    yb_p = fox_prompt(z, c_p, t_p, n_hb, q_col, q_col + n_hb, q_col + 2 * n_hb, FOX_BLK)
    z_s = z[t_p:]
    k_s = z_s[:, c_qb + wb_cols:c_qb + 2 * wb_cols]
    v_s = z_s[:, c_qb + 2 * wb_cols:c_fb]
    yb_s = fox_sample_wrapper(page_table, z_s[:, c_qb:c_qb + wb_cols], k_s, v_s, c_s,
                              cache_k, cache_v, cache_lf, l, bsz, t_new, FOX_SLOTS, FOX_BIAS_PAGES)

    ya = jnp.concatenate([ya_p, ya_s], axis=0)
    yb = jnp.concatenate([yb_p, yb_s.astype(BF16)], axis=0)
    merged = merge_branches(ya, yb, w_proj_a[l].astype(BF16), w_proj_b[l].astype(BF16), zg,
                            0, d_model // TN_MERGE, b_merge[l], TM_MATMUL, TN_MERGE)
    u, ub = out_proj_ln(merged, w_out[l].astype(BF16), x_all, ln1_g[l][None], ln1_b[l][None], alpha, TM_NORM)

    pe = jnp.concatenate([p_prompt[l].reshape(t_p, -1), p_sample[l].reshape(t_s, -1)], axis=0).astype(BF16)
    ple = ple_term(ub, w_ple_gate[l].astype(BF16), pe, w_ple_proj[l].astype(BF16), TM_MATMUL, TN_PLE)
    comb = router(u, w_router[l].T, b_router[l][:, None], TM_ROUTER)
    routed = moe_routed(ub, comb, w_exp_gate[l], w_exp_up[l], w_exp_down[l], TM_MATMUL)
    y = final_ln(u, ub, routed, ple, w_sh_gate[l].astype(BF16), w_sh_up[l].astype(BF16),
                 w_sh_down[l].astype(BF16), ln2_g[l][None], ln2_b[l][None], alpha, TM_NORM)

    k_p = z[:t_p, c_qb + wb_cols:c_qb + 2 * wb_cols]
    v_p = z[:t_p, c_qb + 2 * wb_cols:c_fb]
    return (y[:t_p].reshape(1, t_p, d_model),
            y[t_p:].reshape(bsz, t_new, d_model),
            k_p.reshape(1, 1, t_p, n_hb, dh_b),
            v_p.reshape(1, 1, t_p, n_hb, dh_b),
            lf_p[:, :n_hb].reshape(1, 1, t_p, n_hb),
            s_p[None, None],
            k_s.reshape(1, bsz, t_new, n_hb, dh_b),
            v_s.reshape(1, bsz, t_new, n_hb, dh_b),
            lf_s[:, :n_hb].reshape(1, bsz, t_new, n_hb),
            s_s[None])
```

```python
import functools

import jax
import jax.numpy as jnp
from jax import lax
from jax.experimental import pallas as pl
from jax.experimental.pallas import tpu as pltpu

F32 = jnp.float32
BF16 = jnp.bfloat16

LANES = 128
VMEM_LIMIT = 56 * 1024 * 1024

LN_EPS = 1e-5
NEG_INF = -1e30
ROUTE_SCALE = 2.5
N_GROUPS = 8
TOPK_GROUPS = 4
TOP_K = 8
HGRN_BLK = 16
HGRN_HEADS_PER_ITER = 4

HI = lax.Precision.HIGHEST


def _cparams(*sem):
    return pltpu.CompilerParams(dimension_semantics=sem, vmem_limit_bytes=VMEM_LIMIT)


def _sigmoid(x):
    return 1.0 / (1.0 + jnp.exp(-x))


def _silu(x):
    return x * _sigmoid(x)


def _log_sigmoid(x):
    return jnp.minimum(x, 0.0) - jnp.log1p(jnp.exp(-jnp.abs(x)))


def _dot_nt(a, b, precision=None):
    return lax.dot_general(a, b, (((1,), (1,)), ((), ())), preferred_element_type=F32, precision=precision)


def _dot_tn(a, b, precision=None):
    return lax.dot_general(a, b, (((0,), (0,)), ((), ())), preferred_element_type=F32, precision=precision)


def _split3(x):
    hi = x.astype(BF16)
    rest = x - hi.astype(F32)
    mid = rest.astype(BF16)
    return hi, mid, (rest - mid.astype(F32)).astype(BF16)


def _dot_mask(mask, x):
    mb = mask.astype(BF16)
    hi, mid, lo = _split3(x)
    return (jnp.dot(mb, hi, preferred_element_type=F32) + jnp.dot(mb, mid, preferred_element_type=F32)
            + jnp.dot(mb, lo, preferred_element_type=F32))


def _dot_mask_rhs(x, mask):
    mb = mask.astype(BF16)
    hi, mid, lo = _split3(x)
    return (jnp.dot(hi, mb, preferred_element_type=F32) + jnp.dot(mid, mb, preferred_element_type=F32)
            + jnp.dot(lo, mb, preferred_element_type=F32))


def _in_proj_kernel(x_ref, wt_ref, o_ref, wb_ref):
    @pl.when(pl.program_id(1) == 0)
    def _():
        wb_ref[...] = wt_ref[0].astype(BF16)

    o_ref[...] = _dot_nt(x_ref[...], wb_ref[...])


def in_proj(xb, wt, layer, n_cols, tm, tn):
    m, k = xb.shape
    return pl.pallas_call(
        _in_proj_kernel,
        grid=(n_cols // tn, m // tm),
        in_specs=[pl.BlockSpec((tm, k), lambda j, i: (i, 0)),
                  pl.BlockSpec((1, tn, k), lambda j, i: (layer, j, 0))],
        out_specs=pl.BlockSpec((tm, tn), lambda j, i: (i, j)),
        out_shape=jax.ShapeDtypeStruct((m, n_cols), F32),
        scratch_shapes=[pltpu.VMEM((tn, k), BF16)],
        compiler_params=_cparams("arbitrary", "arbitrary"),
    )(xb, wt)


def _hgrn_head(zq, zf, zi, zg, lb, ng, st, mats, valid, dk):
    rows = zq.shape[0]
    q = _silu(zq) * dk ** -0.5
    logf = jnp.log(lb + (1.0 - lb) * _sigmoid(zf))
    k = (1.0 - lb) * _sigmoid(-zf)
    if valid is not None:
        logf = jnp.where(valid, logf, 0.0)
        k = jnp.where(valid, k, 0.0)
    both = jnp.dot(mats, logf, preferred_element_type=F32, precision=HI)
    b, blast = both[:rows], both[rows:]
    qt = (q * jnp.exp(b)).astype(BF16)
    kt = (k * jnp.exp(blast - b)).astype(BF16)
    vb = zi.astype(BF16)
    trow = lax.broadcasted_iota(jnp.int32, (HGRN_BLK, 1), 0)
    blocks = [slice(i * HGRN_BLK, (i + 1) * HGRN_BLK) for i in range(rows // HGRN_BLK)]
    incs = [_dot_tn(vb[sl], kt[sl]) for sl in blocks]
    states = []
    for sl, inc in zip(blocks, incs):
        states.append(st)
        st = st * jnp.exp(blast[sl.start:sl.start + 1]) + inc
    outs = [_dot_nt(qt[sl], s_in.astype(BF16)) for sl, s_in in zip(blocks, states)]
    for i, sl in enumerate(blocks):
        qb, kb, bb, vv = q[sl], k[sl], b[sl], zi[sl]
        o_blk = outs[i]
        for s in range(HGRN_BLK):
            e = jnp.exp(jnp.minimum(bb - bb[s:s + 1], 0.0))
            col = jnp.sum(qb * kb[s:s + 1] * e, axis=1, keepdims=True)
            o_blk = o_blk + jnp.where(trow >= s, col, 0.0) * vv[s:s + 1]
        outs[i] = o_blk
    o = jnp.concatenate(outs, axis=0)
    o = o * lax.rsqrt(jnp.mean(jnp.square(o), axis=1, keepdims=True) + LN_EPS)
    return o * ng * _silu(zg), st


def _block_mats(rows):
    r = lax.broadcasted_iota(jnp.int32, (rows, rows), 0)
    c = lax.broadcasted_iota(jnp.int32, (rows, rows), 1)
    same = (r // HGRN_BLK) == (c // HGRN_BLK)
    lmat = jnp.where(same & (c <= r), 1.0, 0.0).astype(F32)
    bmat = jnp.where(same, 1.0, 0.0).astype(F32)
    return jnp.concatenate([lmat, bmat], axis=0)


def _lower_bound(lbp_ref, sl):
    p = lbp_ref[:, sl]
    m = jnp.max(p, axis=0, keepdims=True)
    e = jnp.exp(p - m)
    return e[0:1] / jnp.sum(e, axis=0, keepdims=True)


def _hgrn_prompt_kernel(zq_ref, zf_ref, zi_ref, zg_ref, lbp_ref, ng_ref, y_ref, s_ref, st_ref, *, n_heads, dk):
    step = pl.program_id(0)

    @pl.when(step == 0)
    def _():
        st_ref[...] = jnp.zeros_like(st_ref)

    mats = _block_mats(zq_ref.shape[0])

    def head_group(i, carry):
        heads = [i * HGRN_HEADS_PER_ITER + j for j in range(HGRN_HEADS_PER_ITER)]
        lanes = [pl.ds(pl.multiple_of(h * LANES, LANES), LANES) for h in heads]
        args = [(zq_ref[:, sl], zf_ref[:, sl], zi_ref[:, sl], zg_ref[:, sl],
                 _lower_bound(lbp_ref, sl), ng_ref[:, sl], st_ref[h]) for h, sl in zip(heads, lanes)]
        outs = [_hgrn_head(*a, mats, None, dk) for a in args]
        for h, sl, (y, st) in zip(heads, lanes, outs):
            y_ref[:, sl] = y.astype(y_ref.dtype)
            st_ref[h] = st
        return carry

    lax.fori_loop(0, n_heads // HGRN_HEADS_PER_ITER, head_group, 0)

    @pl.when(step == pl.num_programs(0) - 1)
    def _():
        for h in range(n_heads):
            s_ref[h] = st_ref[h].T


def hgrn_prompt(z, lb_param, norm_g, t_len, n_heads, rows):
    width = n_heads * LANES
    kern = functools.partial(_hgrn_prompt_kernel, n_heads=n_heads, dk=LANES)
    return pl.pallas_call(
        kern,
        grid=(t_len // rows,),
        in_specs=[pl.BlockSpec((rows, width), lambda i: (i, 0)),
                  pl.BlockSpec((rows, width), lambda i: (i, 1)),
                  pl.BlockSpec((rows, width), lambda i: (i, 2)),
                  pl.BlockSpec((rows, width), lambda i: (i, 3)),
                  pl.BlockSpec((2, width), lambda i: (0, 0)),
                  pl.BlockSpec((1, width), lambda i: (0, 0))],
        out_specs=[pl.BlockSpec((rows, width), lambda i: (i, 0)),
                   pl.BlockSpec((n_heads, LANES, LANES), lambda i: (0, 0, 0))],
        out_shape=[jax.ShapeDtypeStruct((t_len, width), BF16),
                   jax.ShapeDtypeStruct((n_heads, LANES, LANES), F32)],
        scratch_shapes=[pltpu.VMEM((n_heads, LANES, LANES), F32)],
        compiler_params=_cparams("arbitrary"),
    )(z, z, z, z, lb_param, norm_g)


def _hgrn_sample_kernel(zq_ref, zf_ref, zi_ref, zg_ref, lbp_ref, ng_ref, s0_ref, y_ref, s_ref, *, n_heads, dk, t_new):
    rows = zq_ref.shape[0]
    mats = _block_mats(rows)
    valid = lax.broadcasted_iota(jnp.int32, (rows, 1), 0) < t_new

    def head_group(i, carry):
        heads = [i * HGRN_HEADS_PER_ITER + j for j in range(HGRN_HEADS_PER_ITER)]
        lanes = [pl.ds(pl.multiple_of(h * LANES, LANES), LANES) for h in heads]
        args = [(zq_ref[:, sl], zf_ref[:, sl], zi_ref[:, sl], zg_ref[:, sl],
                 _lower_bound(lbp_ref, sl), ng_ref[:, sl], s0_ref[0, h].T) for h, sl in zip(heads, lanes)]
        outs = [_hgrn_head(*a, mats, valid, dk) for a in args]
        for h, sl, (y, st) in zip(heads, lanes, outs):
            y_ref[:, sl] = y.astype(y_ref.dtype)
            s_ref[0, h] = st.T
        return carry

    lax.fori_loop(0, n_heads // HGRN_HEADS_PER_ITER, head_group, 0)


def hgrn_sample(zs, lb_param, norm_g, s0, n_heads, t_new):
    bsz = s0.shape[0]
    width = n_heads * LANES
    kern = functools.partial(_hgrn_sample_kernel, n_heads=n_heads, dk=LANES, t_new=t_new)
    return pl.pallas_call(
        kern,
        grid=(bsz,),
        in_specs=[pl.BlockSpec((HGRN_BLK, width), lambda b: (b, 0)),
                  pl.BlockSpec((HGRN_BLK, width), lambda b: (b, 1)),
                  pl.BlockSpec((HGRN_BLK, width), lambda b: (b, 2)),
                  pl.BlockSpec((HGRN_BLK, width), lambda b: (b, 3)),
                  pl.BlockSpec((2, width), lambda b: (0, 0)),
                  pl.BlockSpec((1, width), lambda b: (0, 0)),
                  pl.BlockSpec((1, n_heads, LANES, LANES), lambda b: (b, 0, 0, 0))],
        out_specs=[pl.BlockSpec((HGRN_BLK, width), lambda b: (b, 0)),
                   pl.BlockSpec((1, n_heads, LANES, LANES), lambda b: (b, 0, 0, 0))],
        out_shape=[jax.ShapeDtypeStruct((bsz * HGRN_BLK, width), BF16),
                   jax.ShapeDtypeStruct(s0.shape, F32)],
        compiler_params=_cparams("parallel"),
    )(zs, zs, zs, zs, lb_param, norm_g, s0)


def _fox_gate_prompt_kernel(zf_ref, bias_ref, lf_ref, ct_ref, carry_ref):
    @pl.when(pl.program_id(0) == 0)
    def _():
        carry_ref[...] = jnp.zeros_like(carry_ref)

    lf = _log_sigmoid(zf_ref[...] + bias_ref[...])
    lf_ref[...] = lf
    rows = lf.shape[0]
    r = lax.broadcasted_iota(jnp.int32, (rows, rows), 0)
    c = lax.broadcasted_iota(jnp.int32, (rows, rows), 1)
    lower = jnp.where(c <= r, 1.0, 0.0).astype(BF16)
    cs = _dot_mask(lower, lf) + carry_ref[...]
    ct_ref[...] = cs
    carry_ref[...] = cs[rows - 1:rows]


def fox_gate_prompt(zf, bias, t_len, rows):
    return pl.pallas_call(
        _fox_gate_prompt_kernel,
        grid=(t_len // rows,),
        in_specs=[pl.BlockSpec((rows, LANES), lambda i: (i, 0)),
                  pl.BlockSpec((1, LANES), lambda i: (0, 0))],
        out_specs=[pl.BlockSpec((rows, LANES), lambda i: (i, 0)),
                   pl.BlockSpec((rows, LANES), lambda i: (i, 0))],
        out_shape=[jax.ShapeDtypeStruct((t_len, LANES), F32),
                   jax.ShapeDtypeStruct((t_len, LANES), F32)],
        scratch_shapes=[pltpu.VMEM((1, LANES), F32)],
        compiler_params=_cparams("arbitrary"),
    )(zf, bias)


def _fox_gate_sample_kernel(zf_ref, bias_ref, lf_ref, c_ref, *, t_new):
    lf = _log_sigmoid(zf_ref[...] + bias_ref[...])
    lf_ref[...] = lf
    rows = lf.shape[0]
    r = lax.broadcasted_iota(jnp.int32, (rows, rows), 0)
    c = lax.broadcasted_iota(jnp.int32, (rows, rows), 1)
    tri = jnp.where(((r // t_new) == (c // t_new)) & (c <= r), 1.0, 0.0).astype(BF16)
    c_ref[...] = _dot_mask(tri, lf)


def fox_gate_sample(zf, bias, t_new):
    rows = zf.shape[0]
    return pl.pallas_call(
        functools.partial(_fox_gate_sample_kernel, t_new=t_new),
        grid=(1,),
        in_specs=[pl.BlockSpec((rows, LANES), lambda i: (0, 0)),
                  pl.BlockSpec((1, LANES), lambda i: (0, 0))],
        out_specs=[pl.BlockSpec((rows, LANES), lambda i: (0, 0)),
                   pl.BlockSpec((rows, LANES), lambda i: (0, 0))],
        out_shape=[jax.ShapeDtypeStruct((rows, LANES), F32),
                   jax.ShapeDtypeStruct((rows, LANES), F32)],
        compiler_params=_cparams("arbitrary"),
    )(zf, bias)


LOG2E = 1.4426950408889634
FOX_ROW_CHUNKS = 4


def _fox_prompt_kernel(q_ref, k_ref, v_ref, c_ref, o_ref, kb_ref, vb_ref, *, blk, scale):
    h = pl.program_id(0)
    i = pl.program_id(1)
    t_len = k_ref.shape[0]

    @pl.when(i == 0)
    def _():
        lane = lax.broadcasted_iota(jnp.int32, (t_len, LANES), 1)
        beta = -LOG2E * jnp.sum(jnp.where(lane == h, c_ref[...], 0.0), axis=1, keepdims=True)
        hi = beta.astype(BF16).astype(F32)
        mid = (beta - hi).astype(BF16).astype(F32)
        lo = beta - hi - mid
        aug = jnp.where(lane == 0, hi, jnp.where(lane == 1, mid, jnp.where(lane == 2, lo, 0.0)))
        kb_ref[:, :LANES] = k_ref[...].astype(BF16)
        kb_ref[:, LANES:] = aug.astype(BF16)
        vb_ref[:, :LANES] = v_ref[...].astype(BF16)
        vb_ref[:, LANES:] = jnp.where(lane == 0, 1.0, 0.0).astype(BF16)

    qlane = lax.broadcasted_iota(jnp.int32, (blk, LANES), 1)
    q = jnp.concatenate([(q_ref[...] * (scale * LOG2E)).astype(BF16),
                         jnp.where(qlane < 3, 1.0, 0.0).astype(BF16)], axis=1)

    sub = blk // FOX_ROW_CHUNKS
    qs = [q[r * sub:(r + 1) * sub] for r in range(FOX_ROW_CHUNKS)]
    row = lax.broadcasted_iota(jnp.int32, (sub, blk), 0)
    col = lax.broadcasted_iota(jnp.int32, (sub, blk), 1)

    def update(j, state, diagonal):
        off = pl.multiple_of(j * blk, blk)
        kj = kb_ref[pl.ds(off, blk), :]
        vj = vb_ref[pl.ds(off, blk), :]
        out = []
        for r, (m, acc) in enumerate(state):
            s = _dot_nt(qs[r], kj)
            if diagonal:
                s = jnp.where(col <= row + r * sub, s, NEG_INF)
            m_new = jnp.maximum(m, jnp.max(s, axis=1, keepdims=True))
            p = jnp.exp2(s - m_new).astype(BF16)
            out.append((m_new, jnp.exp2(m - m_new) * acc + jnp.dot(p, vj, preferred_element_type=F32)))
        return tuple(out)

    init = tuple((jnp.full((sub, 1), NEG_INF, F32), jnp.zeros((sub, 2 * LANES), F32))
                 for _ in range(FOX_ROW_CHUNKS))
    state = lax.fori_loop(0, i, lambda j, st: update(j, st, False), init)
    state = update(i, state, True)
    for r, (_, acc) in enumerate(state):
        o_ref[r * sub:(r + 1) * sub, :] = (acc[:, :LANES] / acc[:, LANES:LANES + 1]).astype(o_ref.dtype)


def fox_prompt(z, c, t_len, n_heads, q_col, k_col, v_col, blk):
    kern = functools.partial(_fox_prompt_kernel, blk=blk, scale=LANES ** -0.5)
    return pl.pallas_call(
        kern,
        grid=(n_heads, t_len // blk),
        in_specs=[pl.BlockSpec((blk, LANES), lambda h, i: (i, q_col + h)),
                  pl.BlockSpec((t_len, LANES), lambda h, i: (0, k_col + h)),
                  pl.BlockSpec((t_len, LANES), lambda h, i: (0, v_col + h)),
                  pl.BlockSpec((t_len, LANES), lambda h, i: (0, 0))],
        out_specs=pl.BlockSpec((blk, LANES), lambda h, i: (i, h)),
        out_shape=jax.ShapeDtypeStruct((t_len, n_heads * LANES), BF16),
        scratch_shapes=[pltpu.VMEM((t_len, 2 * LANES), BF16), pltpu.VMEM((t_len, 2 * LANES), BF16)],
        compiler_params=_cparams("arbitrary", "arbitrary"),
    )(z, z, z, c)


HEAD_GROUP = 8


def _cache_bias_kernel(lf_ref, o_ref):
    _, n_pages, n_heads, page = lf_ref.shape
    ks = lax.broadcasted_iota(jnp.int32, (page, 2 * page), 0)
    kj = lax.broadcasted_iota(jnp.int32, (page, 2 * page), 1)
    later = jnp.where(ks > kj, 1.0, jnp.where(kj >= page, 1.0, 0.0)).astype(BF16)
    lf = lf_ref[0].reshape(n_pages * n_heads, page)
    o_ref[...] = _dot_mask_rhs(lf, later).reshape(n_pages, n_heads, 2 * page)


def fox_cache_bias(cache_lf_t, layer, pages_per_step):
    depth, pool, n_heads, page = cache_lf_t.shape
    return pl.pallas_call(
        _cache_bias_kernel,
        grid=(pool // pages_per_step,),
        in_specs=[pl.BlockSpec((1, pages_per_step, n_heads, page), lambda i: (layer, i, 0, 0))],
        out_specs=pl.BlockSpec((pages_per_step, n_heads, 2 * page), lambda i: (i, 0, 0)),
        out_shape=jax.ShapeDtypeStruct((pool, n_heads, 2 * page), F32),
        compiler_params=_cparams("parallel"),
    )(cache_lf_t)


def _fox_sample_kernel(pt_ref, q_ref, cnk_ref, kn_ref, vn_ref, *rest, n_slots, t_new):
    k_refs, v_refs, b_refs = rest[:n_slots], rest[n_slots:2 * n_slots], rest[2 * n_slots:3 * n_slots]
    o_ref = rest[3 * n_slots]
    m_ref, l_ref, acc_ref, carry_ref = rest[3 * n_slots + 1:]
    step = pl.program_id(1)
    n_groups = q_ref.shape[1]
    grows = q_ref.shape[2]
    page = k_refs[0].shape[2]

    def own_head(n_keys):
        r = lax.broadcasted_iota(jnp.int32, (grows, n_keys * HEAD_GROUP), 0)
        c = lax.broadcasted_iota(jnp.int32, (grows, n_keys * HEAD_GROUP), 1)
        return r, c, (c % HEAD_GROUP) == (r // t_new)

    def tile(ref, lead, g):
        x = ref[lead + (slice(None), slice(g * HEAD_GROUP, (g + 1) * HEAD_GROUP), slice(None))]
        return x.reshape(x.shape[0] * HEAD_GROUP, x.shape[2]).astype(BF16)

    @pl.when(step == 0)
    def _():
        r, c, same = own_head(2 * t_new)
        visible = same & ((c // HEAD_GROUP) <= (r % t_new))
        for g in range(n_groups):
            s = _dot_nt(q_ref[0, g].astype(BF16), tile(kn_ref, (0,), g))
            s = jnp.where(visible, s - cnk_ref[0, g], NEG_INF)
            m0 = jnp.max(s, axis=1, keepdims=True)
            p0 = jnp.exp(s - m0)
            m_ref[g] = m0
            l_ref[g] = jnp.sum(p0, axis=1, keepdims=True)
            acc_ref[g] = jnp.dot(p0.astype(BF16), tile(vn_ref, (0,), g), preferred_element_type=F32)
        carry_ref[...] = jnp.zeros_like(carry_ref)

    _, _, same = own_head(page)
    other_head = jnp.where(same, 0.0, NEG_INF)
    for g in range(n_groups):
        qg = q_ref[0, g].astype(BF16)
        carry = carry_ref[g]
        parts = []
        for r in range(n_slots):
            bias = other_head + (carry + b_refs[r][0, g, 0:1, :])
            parts.append(_dot_nt(qg, tile(k_refs[r], (0, 0), g)) + bias)
            carry = carry + b_refs[r][0, g, 1:2, :]
        carry_ref[g] = carry
        m = m_ref[g]
        m_new = m
        for s in parts:
            m_new = jnp.maximum(m_new, jnp.max(s, axis=1, keepdims=True))
        alpha = jnp.exp(m - m_new)
        l = alpha * l_ref[g]
        acc = alpha * acc_ref[g]
        for r, s in enumerate(parts):
            p = jnp.exp(s - m_new)
            l = l + jnp.sum(p, axis=1, keepdims=True)
            acc = acc + jnp.dot(p.astype(BF16), tile(v_refs[r], (0, 0), g), preferred_element_type=F32)
        m_ref[g], l_ref[g], acc_ref[g] = m_new, l, acc

    @pl.when(step == pl.num_programs(1) - 1)
    def _():
        for g in range(n_groups):
            o_ref[0, g] = acc_ref[g] / l_ref[g]


def fox_sample(page_table, q, cnk, k_new, v_new, cache_k, cache_v, bias_rows, layer, n_slots, t_new):
    bsz, n_pages = page_table.shape
    _, n_groups, grows, _ = q.shape
    _, _, page, n_heads, dh = cache_k.shape
    cols = page * HEAD_GROUP

    def page_idx(b, g, pt, r):
        return pt[b, n_pages - 1 - (g * n_slots + r)]

    def per_batch(shape):
        return pl.BlockSpec((1,) + shape, lambda b, g, pt: (b,) + (0,) * len(shape))

    in_specs = [per_batch((n_groups, grows, dh)),
                per_batch((n_groups, 1, 2 * t_new * HEAD_GROUP)),
                per_batch((2 * t_new, n_heads, dh)), per_batch((2 * t_new, n_heads, dh))]
    for _ in range(2):
        in_specs += [pl.BlockSpec((1, 1, page, n_heads, dh),
                                  functools.partial(lambda b, g, pt, r: (layer, page_idx(b, g, pt, r), 0, 0, 0), r=r))
                     for r in range(n_slots)]
    in_specs += [pl.BlockSpec((1, n_groups, 2, cols),
                              functools.partial(lambda b, g, pt, r: (page_idx(b, g, pt, r), 0, 0, 0), r=r))
                 for r in range(n_slots)]
    grid_spec = pltpu.PrefetchScalarGridSpec(
        num_scalar_prefetch=1,
        grid=(bsz, n_pages // n_slots),
        in_specs=in_specs,
        out_specs=per_batch((n_groups, grows, dh)),
        scratch_shapes=[pltpu.VMEM((n_groups, grows, 1), F32), pltpu.VMEM((n_groups, grows, 1), F32),
                        pltpu.VMEM((n_groups, grows, dh), F32), pltpu.VMEM((n_groups, 1, cols), F32)],
    )
    kern = functools.partial(_fox_sample_kernel, n_slots=n_slots, t_new=t_new)
    return pl.pallas_call(
        kern,
        grid_spec=grid_spec,
        out_shape=jax.ShapeDtypeStruct(q.shape, F32),
        compiler_params=_cparams("arbitrary", "arbitrary"),
    )(page_table, q, cnk, k_new, v_new,
      *([cache_k] * n_slots), *([cache_v] * n_slots), *([bias_rows] * n_slots))


def fox_sample_wrapper(page_table, zq, zk, zv, c_new, cache_k, cache_v, cache_lf, layer, bsz, t_new, n_slots,
                       bias_pages_per_step):
    _, pool, page, n_heads, dh = cache_k.shape
    n_groups = n_heads // HEAD_GROUP
    grows = HEAD_GROUP * t_new
    q = (zq * dh ** -0.5).reshape(bsz, t_new, n_heads, dh).transpose(0, 2, 1, 3).reshape(bsz, n_groups, grows, dh)
    c = c_new[:, :n_heads].reshape(bsz, t_new, n_groups, HEAD_GROUP)
    cnk = jnp.pad(c.transpose(0, 2, 1, 3), ((0, 0), (0, 0), (0, t_new), (0, 0)))
    cnk = cnk.reshape(bsz, n_groups, 1, 2 * t_new * HEAD_GROUP)
    pad = ((0, 0), (0, t_new), (0, 0), (0, 0))
    k_new = jnp.pad(zk.reshape(bsz, t_new, n_heads, dh), pad)
    v_new = jnp.pad(zv.reshape(bsz, t_new, n_heads, dh), pad)
    bias = fox_cache_bias(jnp.swapaxes(cache_lf, 2, 3), layer, bias_pages_per_step)
    bias_rows = bias.reshape(pool, n_groups, HEAD_GROUP, 2, page).transpose(0, 1, 3, 4, 2)
    bias_rows = bias_rows.reshape(pool, n_groups, 2, page * HEAD_GROUP)
    o = fox_sample(page_table, q, cnk, k_new, v_new, cache_k, cache_v, bias_rows, layer, n_slots, t_new)
    return o.reshape(bsz, n_heads, t_new, dh).transpose(0, 2, 1, 3).reshape(bsz * t_new, n_heads * dh)


def _layer_norm(h, g, b):
    mu = jnp.mean(h, axis=1, keepdims=True)
    d = h - mu
    var = jnp.mean(jnp.square(d), axis=1, keepdims=True)
    return d * lax.rsqrt(var + LN_EPS) * g + b


def _merge_kernel(ya_ref, yb_ref, wa_ref, wb_ref, ga_ref, gb_ref, bm_ref, o_ref):
    a = jnp.dot(ya_ref[...], wa_ref[...], preferred_element_type=F32)
    b = jnp.dot(yb_ref[...], wb_ref[...], preferred_element_type=F32)
    o = _sigmoid(ga_ref[...] + bm_ref[0:1]) * a + _sigmoid(gb_ref[...] + bm_ref[1:2]) * b
    o_ref[...] = o.astype(o_ref.dtype)


def merge_branches(ya, yb, wa, wb, z, ga_col, gb_col, b_merge, tm, tn):
    m, k = ya.shape
    n = wa.shape[1]
    return pl.pallas_call(
        _merge_kernel,
        grid=(m // tm, n // tn),
        in_specs=[pl.BlockSpec((tm, k), lambda i, j: (i, 0)),
                  pl.BlockSpec((tm, k), lambda i, j: (i, 0)),
                  pl.BlockSpec((k, tn), lambda i, j: (0, j)),
                  pl.BlockSpec((k, tn), lambda i, j: (0, j)),
                  pl.BlockSpec((tm, tn), lambda i, j: (i, ga_col + j)),
                  pl.BlockSpec((tm, tn), lambda i, j: (i, gb_col + j)),
                  pl.BlockSpec((2, tn), lambda i, j: (0, j))],
        out_specs=pl.BlockSpec((tm, tn), lambda i, j: (i, j)),
        out_shape=jax.ShapeDtypeStruct((m, n), BF16),
        compiler_params=_cparams("parallel", "parallel"),
    )(ya, yb, wa, wb, z, z, b_merge)


def _out_ln_kernel(m_ref, w_ref, x_ref, g_ref, b_ref, u_ref, ub_ref, *, alpha):
    mix = jnp.dot(m_ref[...], w_ref[...], preferred_element_type=F32)
    u = _layer_norm(alpha * x_ref[...] + mix, g_ref[...], b_ref[...])
    u_ref[...] = u
    ub_ref[...] = u.astype(BF16)


def out_proj_ln(merged, w_out, x, g, b, alpha, tm):
    m, k = merged.shape
    n = w_out.shape[1]
    row = lambda i: (i, 0)
    fixed = lambda i: (0, 0)
    return pl.pallas_call(
        functools.partial(_out_ln_kernel, alpha=alpha),
        grid=(m // tm,),
        in_specs=[pl.BlockSpec((tm, k), row), pl.BlockSpec((k, n), fixed), pl.BlockSpec((tm, n), row),
                  pl.BlockSpec((1, n), fixed), pl.BlockSpec((1, n), fixed)],
        out_specs=[pl.BlockSpec((tm, n), row), pl.BlockSpec((tm, n), row)],
        out_shape=[jax.ShapeDtypeStruct((m, n), F32), jax.ShapeDtypeStruct((m, n), BF16)],
        compiler_params=_cparams("parallel"),
    )(merged, w_out, x, g, b)


def _ple_kernel(u_ref, wg_ref, pe_ref, wp_ref, o_ref):
    gate = _sigmoid(jnp.dot(u_ref[...], wg_ref[...], preferred_element_type=F32))
    proj = jnp.dot(pe_ref[...], wp_ref[...], preferred_element_type=F32)
    o_ref[...] = (gate * proj).astype(o_ref.dtype)


def ple_term(ub, w_gate, pe, w_proj, tm, tn):
    m, k = ub.shape
    n = w_gate.shape[1]
    kp = pe.shape[1]
    return pl.pallas_call(
        _ple_kernel,
        grid=(m // tm, n // tn),
        in_specs=[pl.BlockSpec((tm, k), lambda i, j: (i, 0)),
                  pl.BlockSpec((k, tn), lambda i, j: (0, j)),
                  pl.BlockSpec((tm, kp), lambda i, j: (i, 0)),
                  pl.BlockSpec((kp, tn), lambda i, j: (0, j))],
        out_specs=pl.BlockSpec((tm, tn), lambda i, j: (i, j)),
        out_shape=jax.ShapeDtypeStruct((m, n), BF16),
        compiler_params=_cparams("parallel", "parallel"),
    )(ub, w_gate, pe, w_proj)


def _router_kernel(u_ref, wrt_ref, br_ref, comb_ref, sel_ref):
    scores = _sigmoid(_dot_nt(wrt_ref[...], u_ref[...], precision=HI))
    sel = scores + br_ref[...]
    n_exp, tm = sel.shape
    gsize = n_exp // N_GROUPS
    groups = [sel[g * gsize:(g + 1) * gsize] for g in range(N_GROUPS)]
    rows = []
    for x in groups:
        m1 = jnp.max(x, axis=0, keepdims=True)
        dup = jnp.sum(jnp.where(x == m1, 1.0, 0.0), axis=0, keepdims=True)
        m2 = jnp.max(jnp.where(x < m1, x, -jnp.inf), axis=0, keepdims=True)
        rows.append(m1 + jnp.where(dup >= 2.0, m1, m2))
    gscore = jnp.concatenate(rows, axis=0)
    gidx = lax.broadcasted_iota(jnp.int32, gscore.shape, 0)
    grank = jnp.zeros(gscore.shape, F32)
    for g in range(N_GROUPS):
        o = gscore[g:g + 1]
        grank = grank + jnp.where(o > gscore, 1.0, jnp.where(o == gscore, jnp.where(g < gidx, 1.0, 0.0), 0.0))
    masked = jnp.concatenate(
        [jnp.where(grank[g:g + 1] < TOPK_GROUPS, groups[g], NEG_INF) for g in range(N_GROUPS)], axis=0)
    sel_ref[...] = masked
    eidx = lax.broadcasted_iota(jnp.int32, masked.shape, 0)

    def count_better(e, rank):
        o = sel_ref[pl.ds(e, 1), :]
        return rank + jnp.where(o > masked, 1.0, jnp.where(o == masked, jnp.where(e < eidx, 1.0, 0.0), 0.0))

    rank = lax.fori_loop(0, n_exp, count_better, jnp.zeros(masked.shape, F32))
    w = jnp.where(rank < TOP_K, scores, 0.0)
    w = w / jnp.sum(w, axis=0, keepdims=True) * ROUTE_SCALE
    pad = jnp.zeros((comb_ref.shape[1] - n_exp, tm), F32)
    comb_ref[...] = jnp.concatenate([w, pad], axis=0).T


def router(u, w_router_t, b_router, tm):
    m, k = u.shape
    n_exp = w_router_t.shape[0]
    return pl.pallas_call(
        _router_kernel,
        grid=(m // tm,),
        in_specs=[pl.BlockSpec((tm, k), lambda i: (i, 0)),
                  pl.BlockSpec((n_exp, k), lambda i: (0, 0)),
                  pl.BlockSpec((n_exp, 1), lambda i: (0, 0))],
        out_specs=pl.BlockSpec((tm, LANES), lambda i: (i, 0)),
        out_shape=jax.ShapeDtypeStruct((m, LANES), F32),
        scratch_shapes=[pltpu.VMEM((n_exp, tm), F32)],
        compiler_params=_cparams("parallel"),
    )(u, w_router_t, b_router)


def _moe_kernel(u_ref, comb_ref, wg_ref, wu_ref, wd_ref, o_ref):
    e = pl.program_id(1)

    @pl.when(e == 0)
    def _():
        o_ref[...] = jnp.zeros_like(o_ref)

    u = u_ref[...]
    hg = jnp.dot(u, wg_ref[0].astype(BF16), preferred_element_type=F32)
    hu = jnp.dot(u, wu_ref[0].astype(BF16), preferred_element_type=F32)
    comb = comb_ref[...]
    lane = lax.broadcasted_iota(jnp.int32, comb.shape, 1)
    c = jnp.sum(jnp.where(lane == e, comb, 0.0), axis=1, keepdims=True)
    h = (_silu(hg) * hu * c).astype(BF16)
    o_ref[...] += jnp.dot(h, wd_ref[0].astype(BF16), preferred_element_type=F32)


def moe_routed(ub, comb, w_gate, w_up, w_down, tm):
    m, k = ub.shape
    n_exp, _, f = w_gate.shape
    return pl.pallas_call(
        _moe_kernel,
        grid=(m // tm, n_exp),
        in_specs=[pl.BlockSpec((tm, k), lambda i, e: (i, 0)),
                  pl.BlockSpec((tm, LANES), lambda i, e: (i, 0)),
                  pl.BlockSpec((1, k, f), lambda i, e: (e, 0, 0)),
                  pl.BlockSpec((1, k, f), lambda i, e: (e, 0, 0)),
                  pl.BlockSpec((1, f, k), lambda i, e: (e, 0, 0))],
        out_specs=pl.BlockSpec((tm, k), lambda i, e: (i, 0)),
        out_shape=jax.ShapeDtypeStruct((m, k), F32),
        compiler_params=_cparams("parallel", "arbitrary"),
    )(ub, comb, w_gate, w_up, w_down)


def _final_kernel(u_ref, ub_ref, routed_ref, ple_ref, wsg_ref, wsu_ref, wsd_ref, g_ref, b_ref, y_ref, *, alpha):
    ub = ub_ref[...]
    hs = _silu(jnp.dot(ub, wsg_ref[...], preferred_element_type=F32)) * jnp.dot(
        ub, wsu_ref[...], preferred_element_type=F32)
    shared = jnp.dot(hs.astype(BF16), wsd_ref[...], preferred_element_type=F32)
    h = alpha * u_ref[...] + (routed_ref[...] + shared) + ple_ref[...].astype(F32)
    y_ref[...] = _layer_norm(h, g_ref[...], b_ref[...])


def final_ln(u, ub, routed, ple, wsg, wsu, wsd, g, b, alpha, tm):
    m, n = u.shape
    f = wsg.shape[1]
    row = lambda i: (i, 0)
    fixed = lambda i: (0, 0)
    return pl.pallas_call(
        functools.partial(_final_kernel, alpha=alpha),
        grid=(m // tm,),
        in_specs=[pl.BlockSpec((tm, n), row), pl.BlockSpec((tm, n), row), pl.BlockSpec((tm, n), row),
                  pl.BlockSpec((tm, n), row), pl.BlockSpec((n, f), fixed), pl.BlockSpec((n, f), fixed),
                  pl.BlockSpec((f, n), fixed), pl.BlockSpec((1, n), fixed), pl.BlockSpec((1, n), fixed)],
        out_specs=pl.BlockSpec((tm, n), row),
        out_shape=jax.ShapeDtypeStruct((m, n), F32),
        compiler_params=_cparams("parallel"),
    )(u, ub, routed, ple, wsg, wsu, wsd, g, b)


TM_MATMUL = 832
TM_NORM = 416
TM_ROUTER = 640
TN_IN = 1024
TN_MERGE = 512
TN_PLE = 1024
HGRN_ROWS = 128
FOX_GATE_ROWS = 256
FOX_BLK = 1024
FOX_SLOTS = 8
FOX_BIAS_PAGES = 64


def kernel(x_prompt, x_sample, p_prompt, p_sample, cache_k, cache_v, cache_lf, state_hgrn, page_table, w_in, b_fox_f, b_merge, lb_param, hgrn_norm_g, w_proj_a, w_proj_b, w_out, ln1_g, ln1_b, ln2_g, ln2_b, w_router, b_router, w_exp_gate, w_exp_up, w_exp_down, w_sh_gate, w_sh_up, w_sh_down, w_ple_proj, w_ple_gate):
    depth, d_model, _ = w_in.shape
    assert depth == 1 and x_prompt.shape[0] == 1, "single layer, single prompt sequence"
    t_p = x_prompt.shape[1]
    bsz, t_new, _ = x_sample.shape
    t_s = bsz * t_new
    n_ha, dk_a, dv_a = state_hgrn.shape[2:]
    _, pool, page, n_hb, dh_b = cache_k.shape
    assert dk_a == LANES and dv_a == LANES and dh_b == LANES and n_hb % 2 == 0
    wa_cols = n_ha * LANES
    wb_cols = n_hb * LANES
    c_qb = 4 * wa_cols
    c_fb = c_qb + 3 * wb_cols
    alpha = (2.0 * depth) ** 0.25
    l = 0

    x_all = jnp.concatenate([x_prompt.reshape(t_p, d_model), x_sample.reshape(t_s, d_model)], axis=0)
    xb = x_all.astype(BF16)
    wt_in = jnp.swapaxes(w_in, 1, 2)
    wt_gates = wt_in[l:l + 1, c_fb + n_hb:]
    wt_fb = jnp.pad(wt_in[l:l + 1, c_fb:c_fb + n_hb], ((0, 0), (0, LANES - n_hb), (0, 0)))
    z = in_proj(xb, wt_in, l, c_fb, TM_MATMUL, TN_IN)
    zg = in_proj(xb, wt_gates, 0, 2 * d_model, TM_MATMUL, TN_IN)
    zf = in_proj(xb, wt_fb, 0, LANES, TM_MATMUL, LANES)

    ng = hgrn_norm_g[l][None]
    ya_p, s_p = hgrn_prompt(z, lb_param, ng, t_p, n_ha, HGRN_ROWS)
    zs = jnp.pad(z[t_p:, :c_qb].reshape(bsz, t_new, c_qb), ((0, 0), (0, HGRN_BLK - t_new), (0, 0)))
    ya_s, s_s = hgrn_sample(zs.reshape(bsz * HGRN_BLK, c_qb), lb_param, ng, state_hgrn[l], n_ha, t_new)
    ya_s = ya_s.reshape(bsz, HGRN_BLK, wa_cols)[:, :t_new].reshape(t_s, wa_cols)

    fbias = jnp.pad(b_fox_f[l], (0, LANES - n_hb))[None]
    lf_p, c_p = fox_gate_prompt(zf, fbias, t_p, FOX_GATE_ROWS)
    lf_s, c_s = fox_gate_sample(zf[t_p:], fbias, t_new)
    q_col = c_qb // LANES
    yb_p = fox_prompt(z, c_p, t_p, n_hb, q_col, q_col + n_hb, q_col + 2 * n_hb, FOX_BLK)
    z_s = z[t_p:]
    k_s = z_s[:, c_qb + wb_cols:c_qb + 2 * wb_cols]
    v_s = z_s[:, c_qb + 2 * wb_cols:c_fb]
    yb_s = fox_sample_wrapper(page_table, z_s[:, c_qb:c_qb + wb_cols], k_s, v_s, c_s,
                              cache_k, cache_v, cache_lf, l, bsz, t_new, FOX_SLOTS, FOX_BIAS_PAGES)

    ya = jnp.concatenate([ya_p, ya_s], axis=0)
    yb = jnp.concatenate([yb_p, yb_s.astype(BF16)], axis=0)
    merged = merge_branches(ya, yb, w_proj_a[l].astype(BF16), w_proj_b[l].astype(BF16), zg,
                            0, d_model // TN_MERGE, b_merge[l], TM_MATMUL, TN_MERGE)
    u, ub = out_proj_ln(merged, w_out[l].astype(BF16), x_all, ln1_g[l][None], ln1_b[l][None], alpha, TM_NORM)

    pe = jnp.concatenate([p_prompt[l].reshape(t_p, -1), p_sample[l].reshape(t_s, -1)], axis=0).astype(BF16)
    ple = ple_term(ub, w_ple_gate[l].astype(BF16), pe, w_ple_proj[l].astype(BF16), TM_MATMUL, TN_PLE)
    comb = router(u, w_router[l].T, b_router[l][:, None], TM_ROUTER)
    routed = moe_routed(ub, comb, w_exp_gate[l], w_exp_up[l], w_exp_down[l], TM_MATMUL)
    y = final_ln(u, ub, routed, ple, w_sh_gate[l].astype(BF16), w_sh_up[l].astype(BF16),
                 w_sh_down[l].astype(BF16), ln2_g[l][None], ln2_b[l][None], alpha, TM_NORM)

    k_p = z[:t_p, c_qb + wb_cols:c_qb + 2 * wb_cols]
    v_p = z[:t_p, c_qb + 2 * wb_cols:c_fb]
    return (y[:t_p].reshape(1, t_p, d_model),
            y[t_p:].reshape(bsz, t_new, d_model),
            k_p.reshape(1, 1, t_p, n_hb, dh_b),
            v_p.reshape(1, 1, t_p, n_hb, dh_b),
            lf_p[:, :n_hb].reshape(1, 1, t_p, n_hb),
            s_p[None, None],
            k_s.reshape(1, bsz, t_new, n_hb, dh_b),
            v_s.reshape(1, bsz, t_new, n_hb, dh_b),
            lf_s[:, :n_hb].reshape(1, bsz, t_new, n_hb),
            s_s[None])
```

```python
import functools

import jax
import jax.numpy as jnp
from jax import lax
from jax.experimental import pallas as pl
from jax.experimental.pallas import tpu as pltpu

F32 = jnp.float32
BF16 = jnp.bfloat16

LANES = 128
VMEM_LIMIT = 56 * 1024 * 1024

LN_EPS = 1e-5
NEG_INF = -1e30
ROUTE_SCALE = 2.5
N_GROUPS = 8
TOPK_GROUPS = 4
TOP_K = 8
HGRN_BLK = 16
HGRN_HEADS_PER_ITER = 4

HI = lax.Precision.HIGHEST


def _cparams(*sem):
    return pltpu.CompilerParams(dimension_semantics=sem, vmem_limit_bytes=VMEM_LIMIT)


def _sigmoid(x):
    return 1.0 / (1.0 + jnp.exp(-x))


def _silu(x):
    return x * _sigmoid(x)


def _log_sigmoid(x):
    return jnp.minimum(x, 0.0) - jnp.log1p(jnp.exp(-jnp.abs(x)))


def _dot_nt(a, b, precision=None):
    return lax.dot_general(a, b, (((1,), (1,)), ((), ())), preferred_element_type=F32, precision=precision)


def _dot_tn(a, b, precision=None):
    return lax.dot_general(a, b, (((0,), (0,)), ((), ())), preferred_element_type=F32, precision=precision)


def _split3(x):
    hi = x.astype(BF16)
    rest = x - hi.astype(F32)
    mid = rest.astype(BF16)
    return hi, mid, (rest - mid.astype(F32)).astype(BF16)


def _dot_mask(mask, x):
    mb = mask.astype(BF16)
    hi, mid, lo = _split3(x)
    return (jnp.dot(mb, hi, preferred_element_type=F32) + jnp.dot(mb, mid, preferred_element_type=F32)
            + jnp.dot(mb, lo, preferred_element_type=F32))


def _dot_mask_rhs(x, mask):
    mb = mask.astype(BF16)
    hi, mid, lo = _split3(x)
    return (jnp.dot(hi, mb, preferred_element_type=F32) + jnp.dot(mid, mb, preferred_element_type=F32)
            + jnp.dot(lo, mb, preferred_element_type=F32))


def _in_proj_kernel(x_ref, wt_ref, o_ref, wb_ref):
    @pl.when(pl.program_id(1) == 0)
    def _():
        wb_ref[...] = wt_ref[0].astype(BF16)

    o_ref[...] = _dot_nt(x_ref[...].astype(BF16), wb_ref[...])


def in_proj(x, wt, layer, row0, n_cols, tm, tn):
    m, k = x.shape
    return pl.pallas_call(
        _in_proj_kernel,
        grid=(n_cols // tn, m // tm),
        in_specs=[pl.BlockSpec((tm, k), lambda j, i: (i, 0)),
                  pl.BlockSpec((pl.Element(1), pl.Element(tn), pl.Element(k)),
                               lambda j, i: (layer, pl.multiple_of(row0 + j * tn, 8), 0))],
        out_specs=pl.BlockSpec((tm, tn), lambda j, i: (i, j)),
        out_shape=jax.ShapeDtypeStruct((m, n_cols), F32),
        scratch_shapes=[pltpu.VMEM((tn, k), BF16)],
        compiler_params=_cparams("arbitrary", "arbitrary"),
    )(x, wt)


def _hgrn_head(zq, zf, zi, zg, lb, ng, st, mats, n_keys, dk):
    rows = zq.shape[0]
    q = _silu(zq) * dk ** -0.5
    logf = jnp.log(lb + (1.0 - lb) * _sigmoid(zf))
    k = (1.0 - lb) * _sigmoid(-zf)
    if n_keys < HGRN_BLK:
        valid = lax.broadcasted_iota(jnp.int32, (rows, 1), 0) % HGRN_BLK < n_keys
        logf = jnp.where(valid, logf, 0.0)
        k = jnp.where(valid, k, 0.0)
    both = jnp.dot(mats, logf, preferred_element_type=F32, precision=HI)
    b, blast = both[:rows], both[rows:]
    qt = (q * jnp.exp(b)).astype(BF16)
    kt = (k * jnp.exp(blast - b)).astype(BF16)
    vb = zi.astype(BF16)
    trow = lax.broadcasted_iota(jnp.int32, (HGRN_BLK, 1), 0)
    blocks = [slice(i * HGRN_BLK, (i + 1) * HGRN_BLK) for i in range(rows // HGRN_BLK)]
    incs = [_dot_tn(vb[sl], kt[sl]) for sl in blocks]
    states = []
    for sl, inc in zip(blocks, incs):
        states.append(st)
        st = st * jnp.exp(blast[sl.start:sl.start + 1]) + inc
    outs = [_dot_nt(qt[sl], s_in.astype(BF16)) for sl, s_in in zip(blocks, states)]
    for i, sl in enumerate(blocks):
        qb, kb, bb, vv = q[sl], k[sl], b[sl], zi[sl]
        o_blk = outs[i]
        for s in range(n_keys):
            e = jnp.exp(jnp.minimum(bb - bb[s:s + 1], 0.0))
            col = jnp.sum(qb * kb[s:s + 1] * e, axis=1, keepdims=True)
            o_blk = o_blk + jnp.where(trow >= s, col, 0.0) * vv[s:s + 1]
        outs[i] = o_blk
    o = jnp.concatenate(outs, axis=0)
    o = o * lax.rsqrt(jnp.mean(jnp.square(o), axis=1, keepdims=True) + LN_EPS)
    return o * ng * _silu(zg), st


def _block_mats(rows):
    r = lax.broadcasted_iota(jnp.int32, (rows, rows), 0)
    c = lax.broadcasted_iota(jnp.int32, (rows, rows), 1)
    same = (r // HGRN_BLK) == (c // HGRN_BLK)
    lmat = jnp.where(same & (c <= r), 1.0, 0.0).astype(F32)
    bmat = jnp.where(same, 1.0, 0.0).astype(F32)
    return jnp.concatenate([lmat, bmat], axis=0)


def _lower_bound(lbp_ref, sl):
    p = lbp_ref[:, sl]
    m = jnp.max(p, axis=0, keepdims=True)
    e = jnp.exp(p - m)
    return e[0:1] / jnp.sum(e, axis=0, keepdims=True)


def _hgrn_prompt_kernel(zq_ref, zf_ref, zi_ref, zg_ref, lbp_ref, ng_ref, y_ref, s_ref, st_ref, *, n_heads, dk):
    step = pl.program_id(0)

    @pl.when(step == 0)
    def _():
        st_ref[...] = jnp.zeros_like(st_ref)

    mats = _block_mats(zq_ref.shape[0])

    def head_group(i, carry):
        heads = [i * HGRN_HEADS_PER_ITER + j for j in range(HGRN_HEADS_PER_ITER)]
        lanes = [pl.ds(pl.multiple_of(h * LANES, LANES), LANES) for h in heads]
        args = [(zq_ref[:, sl], zf_ref[:, sl], zi_ref[:, sl], zg_ref[:, sl],
                 _lower_bound(lbp_ref, sl), ng_ref[:, sl], st_ref[h]) for h, sl in zip(heads, lanes)]
        outs = [_hgrn_head(*a, mats, HGRN_BLK, dk) for a in args]
        for h, sl, (y, st) in zip(heads, lanes, outs):
            y_ref[:, sl] = y.astype(y_ref.dtype)
            st_ref[h] = st
        return carry

    lax.fori_loop(0, n_heads // HGRN_HEADS_PER_ITER, head_group, 0)

    @pl.when(step == pl.num_programs(0) - 1)
    def _():
        for h in range(n_heads):
            s_ref[h] = st_ref[h].T


def hgrn_prompt(z, lb_param, norm_g, t_len, n_heads, rows):
    width = n_heads * LANES
    kern = functools.partial(_hgrn_prompt_kernel, n_heads=n_heads, dk=LANES)
    return pl.pallas_call(
        kern,
        grid=(t_len // rows,),
        in_specs=[pl.BlockSpec((rows, width), lambda i: (i, 0)),
                  pl.BlockSpec((rows, width), lambda i: (i, 1)),
                  pl.BlockSpec((rows, width), lambda i: (i, 2)),
                  pl.BlockSpec((rows, width), lambda i: (i, 3)),
                  pl.BlockSpec((2, width), lambda i: (0, 0)),
                  pl.BlockSpec((1, width), lambda i: (0, 0))],
        out_specs=[pl.BlockSpec((rows, width), lambda i: (i, 0)),
                   pl.BlockSpec((n_heads, LANES, LANES), lambda i: (0, 0, 0))],
        out_shape=[jax.ShapeDtypeStruct((t_len, width), BF16),
                   jax.ShapeDtypeStruct((n_heads, LANES, LANES), F32)],
        scratch_shapes=[pltpu.VMEM((n_heads, LANES, LANES), F32)],
        compiler_params=_cparams("arbitrary"),
    )(z, z, z, z, lb_param, norm_g)


def _hgrn_sample_kernel(zq_ref, zf_ref, zi_ref, zg_ref, lbp_ref, ng_ref, s0_ref, y_ref, s_ref, *, n_heads, dk, t_new):
    rows = zq_ref.shape[0]
    mats = _block_mats(rows)

    def head_group(i, carry):
        heads = [i * HGRN_HEADS_PER_ITER + j for j in range(HGRN_HEADS_PER_ITER)]
        lanes = [pl.ds(pl.multiple_of(h * LANES, LANES), LANES) for h in heads]
        args = [(zq_ref[:, sl], zf_ref[:, sl], zi_ref[:, sl], zg_ref[:, sl],
                 _lower_bound(lbp_ref, sl), ng_ref[:, sl], s0_ref[0, h].T) for h, sl in zip(heads, lanes)]
        outs = [_hgrn_head(*a, mats, t_new, dk) for a in args]
        for h, sl, (y, st) in zip(heads, lanes, outs):
            y_ref[:, sl] = y.astype(y_ref.dtype)
            s_ref[0, h] = st.T
        return carry

    lax.fori_loop(0, n_heads // HGRN_HEADS_PER_ITER, head_group, 0)


def hgrn_sample(zs, lb_param, norm_g, s0, n_heads, t_new):
    bsz = s0.shape[0]
    width = n_heads * LANES
    kern = functools.partial(_hgrn_sample_kernel, n_heads=n_heads, dk=LANES, t_new=t_new)
    return pl.pallas_call(
        kern,
        grid=(bsz,),
        in_specs=[pl.BlockSpec((HGRN_BLK, width), lambda b: (b, 0)),
                  pl.BlockSpec((HGRN_BLK, width), lambda b: (b, 1)),
                  pl.BlockSpec((HGRN_BLK, width), lambda b: (b, 2)),
                  pl.BlockSpec((HGRN_BLK, width), lambda b: (b, 3)),
                  pl.BlockSpec((2, width), lambda b: (0, 0)),
                  pl.BlockSpec((1, width), lambda b: (0, 0)),
                  pl.BlockSpec((1, n_heads, LANES, LANES), lambda b: (b, 0, 0, 0))],
        out_specs=[pl.BlockSpec((HGRN_BLK, width), lambda b: (b, 0)),
                   pl.BlockSpec((1, n_heads, LANES, LANES), lambda b: (b, 0, 0, 0))],
        out_shape=[jax.ShapeDtypeStruct((bsz * HGRN_BLK, width), BF16),
                   jax.ShapeDtypeStruct(s0.shape, F32)],
        compiler_params=_cparams("parallel"),
    )(zs, zs, zs, zs, lb_param, norm_g, s0)


def _fox_gate_prompt_kernel(zf_ref, bias_ref, lf_ref, ct_ref, carry_ref):
    @pl.when(pl.program_id(0) == 0)
    def _():
        carry_ref[...] = jnp.zeros_like(carry_ref)

    lf = _log_sigmoid(zf_ref[...] + bias_ref[...])
    lf_ref[...] = lf
    rows = lf.shape[0]
    r = lax.broadcasted_iota(jnp.int32, (rows, rows), 0)
    c = lax.broadcasted_iota(jnp.int32, (rows, rows), 1)
    lower = jnp.where(c <= r, 1.0, 0.0).astype(BF16)
    cs = _dot_mask(lower, lf) + carry_ref[...]
    ct_ref[...] = cs
    carry_ref[...] = cs[rows - 1:rows]


def fox_gate_prompt(zf, bias, t_len, rows):
    return pl.pallas_call(
        _fox_gate_prompt_kernel,
        grid=(t_len // rows,),
        in_specs=[pl.BlockSpec((rows, LANES), lambda i: (i, 0)),
                  pl.BlockSpec((1, LANES), lambda i: (0, 0))],
        out_specs=[pl.BlockSpec((rows, LANES), lambda i: (i, 0)),
                   pl.BlockSpec((rows, LANES), lambda i: (i, 0))],
        out_shape=[jax.ShapeDtypeStruct((t_len, LANES), F32),
                   jax.ShapeDtypeStruct((t_len, LANES), F32)],
        scratch_shapes=[pltpu.VMEM((1, LANES), F32)],
        compiler_params=_cparams("arbitrary"),
    )(zf, bias)


def _fox_gate_sample_kernel(zf_ref, bias_ref, lf_ref, c_ref, *, t_new):
    lf = _log_sigmoid(zf_ref[...] + bias_ref[...])
    lf_ref[...] = lf
    rows = lf.shape[0]
    r = lax.broadcasted_iota(jnp.int32, (rows, rows), 0)
    c = lax.broadcasted_iota(jnp.int32, (rows, rows), 1)
    tri = jnp.where(((r // t_new) == (c // t_new)) & (c <= r), 1.0, 0.0).astype(BF16)
    c_ref[...] = _dot_mask(tri, lf)


def fox_gate_sample(zf, bias, t_new):
    rows = zf.shape[0]
    return pl.pallas_call(
        functools.partial(_fox_gate_sample_kernel, t_new=t_new),
        grid=(1,),
        in_specs=[pl.BlockSpec((rows, LANES), lambda i: (0, 0)),
                  pl.BlockSpec((1, LANES), lambda i: (0, 0))],
        out_specs=[pl.BlockSpec((rows, LANES), lambda i: (0, 0)),
                   pl.BlockSpec((rows, LANES), lambda i: (0, 0))],
        out_shape=[jax.ShapeDtypeStruct((rows, LANES), F32),
                   jax.ShapeDtypeStruct((rows, LANES), F32)],
        compiler_params=_cparams("arbitrary"),
    )(zf, bias)


LOG2E = 1.4426950408889634
FOX_ROW_CHUNKS = 4


def _fox_prompt_kernel(q_ref, k_ref, v_ref, c_ref, o_ref, kb_ref, vb_ref, *, blk, scale):
    h = pl.program_id(0)
    i = pl.program_id(1)
    t_len = k_ref.shape[0]

    @pl.when(i == 0)
    def _():
        lane = lax.broadcasted_iota(jnp.int32, (t_len, LANES), 1)
        beta = -LOG2E * jnp.sum(jnp.where(lane == h, c_ref[...], 0.0), axis=1, keepdims=True)
        hi = beta.astype(BF16).astype(F32)
        mid = (beta - hi).astype(BF16).astype(F32)
        lo = beta - hi - mid
        aug = jnp.where(lane == 0, hi, jnp.where(lane == 1, mid, jnp.where(lane == 2, lo, 0.0)))
        kb_ref[:, :LANES] = k_ref[...].astype(BF16)
        kb_ref[:, LANES:] = aug.astype(BF16)
        vb_ref[:, :LANES] = v_ref[...].astype(BF16)
        vb_ref[:, LANES:] = jnp.where(lane == 0, 1.0, 0.0).astype(BF16)

    qlane = lax.broadcasted_iota(jnp.int32, (blk, LANES), 1)
    q = jnp.concatenate([(q_ref[...] * (scale * LOG2E)).astype(BF16),
                         jnp.where(qlane < 3, 1.0, 0.0).astype(BF16)], axis=1)

    sub = blk // FOX_ROW_CHUNKS
    qs = [q[r * sub:(r + 1) * sub] for r in range(FOX_ROW_CHUNKS)]
    row = lax.broadcasted_iota(jnp.int32, (sub, blk), 0)
    col = lax.broadcasted_iota(jnp.int32, (sub, blk), 1)

    def update(j, state, diagonal):
        off = pl.multiple_of(j * blk, blk)
        kj = kb_ref[pl.ds(off, blk), :]
        vj = vb_ref[pl.ds(off, blk), :]
        scored = [_dot_nt(qr, kj) for qr in qs]
        out = []
        for r, (s, (m, acc)) in enumerate(zip(scored, state)):
            if diagonal:
                s = jnp.where(col <= row + r * sub, s, NEG_INF)
            m_new = jnp.maximum(m, jnp.max(s, axis=1, keepdims=True))
            p = jnp.exp2(s - m_new).astype(BF16)
            out.append((m_new, jnp.exp2(m - m_new) * acc + jnp.dot(p, vj, preferred_element_type=F32)))
        return tuple(out)

    init = tuple((jnp.full((sub, 1), NEG_INF, F32), jnp.zeros((sub, 2 * LANES), F32))
                 for _ in range(FOX_ROW_CHUNKS))
    state = lax.fori_loop(0, i, lambda j, st: update(j, st, False), init)
    state = update(i, state, True)
    for r, (_, acc) in enumerate(state):
        o_ref[r * sub:(r + 1) * sub, :] = (acc[:, :LANES] / acc[:, LANES:LANES + 1]).astype(o_ref.dtype)


def fox_prompt(z, c, t_len, n_heads, q_col, k_col, v_col, blk):
    kern = functools.partial(_fox_prompt_kernel, blk=blk, scale=LANES ** -0.5)
    return pl.pallas_call(
        kern,
        grid=(n_heads, t_len // blk),
        in_specs=[pl.BlockSpec((blk, LANES), lambda h, i: (i, q_col + h)),
                  pl.BlockSpec((t_len, LANES), lambda h, i: (0, k_col + h)),
                  pl.BlockSpec((t_len, LANES), lambda h, i: (0, v_col + h)),
                  pl.BlockSpec((t_len, LANES), lambda h, i: (0, 0))],
        out_specs=pl.BlockSpec((blk, LANES), lambda h, i: (i, h)),
        out_shape=jax.ShapeDtypeStruct((t_len, n_heads * LANES), BF16),
        scratch_shapes=[pltpu.VMEM((t_len, 2 * LANES), BF16), pltpu.VMEM((t_len, 2 * LANES), BF16)],
        compiler_params=_cparams("arbitrary", "arbitrary"),
    )(z, z, z, c)


HEAD_GROUP = 8


def _cache_bias_kernel(lf_ref, o_ref):
    _, n_pages, n_heads, page = lf_ref.shape
    ks = lax.broadcasted_iota(jnp.int32, (page, 2 * page), 0)
    kj = lax.broadcasted_iota(jnp.int32, (page, 2 * page), 1)
    later = jnp.where(ks > kj, 1.0, jnp.where(kj >= page, 1.0, 0.0)).astype(BF16)
    lf = lf_ref[0].reshape(n_pages * n_heads, page)
    o_ref[...] = _dot_mask_rhs(lf, later).reshape(n_pages, n_heads, 2 * page)


def fox_cache_bias(cache_lf_t, layer, pages_per_step):
    depth, pool, n_heads, page = cache_lf_t.shape
    return pl.pallas_call(
        _cache_bias_kernel,
        grid=(pool // pages_per_step,),
        in_specs=[pl.BlockSpec((1, pages_per_step, n_heads, page), lambda i: (layer, i, 0, 0))],
        out_specs=pl.BlockSpec((pages_per_step, n_heads, 2 * page), lambda i: (i, 0, 0)),
        out_shape=jax.ShapeDtypeStruct((pool, n_heads, 2 * page), F32),
        compiler_params=_cparams("parallel"),
    )(cache_lf_t)


def _fox_sample_kernel(pt_ref, q_ref, cnk_ref, kn_ref, vn_ref, *rest, n_slots, t_new):
    k_refs, v_refs, b_refs = rest[:n_slots], rest[n_slots:2 * n_slots], rest[2 * n_slots:3 * n_slots]
    o_ref = rest[3 * n_slots]
    m_ref, l_ref, acc_ref, carry_ref = rest[3 * n_slots + 1:]
    step = pl.program_id(1)
    n_groups = q_ref.shape[1]
    grows = q_ref.shape[2]
    page = k_refs[0].shape[2]

    def own_head(n_keys):
        r = lax.broadcasted_iota(jnp.int32, (grows, n_keys * HEAD_GROUP), 0)
        c = lax.broadcasted_iota(jnp.int32, (grows, n_keys * HEAD_GROUP), 1)
        return r, c, (c % HEAD_GROUP) == (r // t_new)

    def tile(ref, lead, g):
        x = ref[lead + (slice(None), slice(g * HEAD_GROUP, (g + 1) * HEAD_GROUP), slice(None))]
        return x.reshape(x.shape[0] * HEAD_GROUP, x.shape[2]).astype(BF16)

    @pl.when(step == 0)
    def _():
        r, c, same = own_head(2 * t_new)
        visible = same & ((c // HEAD_GROUP) <= (r % t_new))
        for g in range(n_groups):
            s = _dot_nt(q_ref[0, g].astype(BF16), tile(kn_ref, (0,), g))
            s = jnp.where(visible, s - cnk_ref[0, g], NEG_INF)
            m0 = jnp.max(s, axis=1, keepdims=True)
            p0 = jnp.exp(s - m0)
            m_ref[g] = m0
            l_ref[g] = jnp.sum(p0, axis=1, keepdims=True)
            acc_ref[g] = jnp.dot(p0.astype(BF16), tile(vn_ref, (0,), g), preferred_element_type=F32)
        carry_ref[...] = jnp.zeros_like(carry_ref)

    _, _, same = own_head(page)
    other_head = jnp.where(same, 0.0, NEG_INF)
    for g in range(n_groups):
        qg = q_ref[0, g].astype(BF16)
        carry = carry_ref[g]
        parts = []
        for r in range(n_slots):
            bias = other_head + (carry + b_refs[r][0, g, 0:1, :])
            parts.append(_dot_nt(qg, tile(k_refs[r], (0, 0), g)) + bias)
            carry = carry + b_refs[r][0, g, 1:2, :]
        carry_ref[g] = carry
        m = m_ref[g]
        m_new = m
        for s in parts:
            m_new = jnp.maximum(m_new, jnp.max(s, axis=1, keepdims=True))
        alpha = jnp.exp(m - m_new)
        l = alpha * l_ref[g]
        acc = alpha * acc_ref[g]
        for r, s in enumerate(parts):
            p = jnp.exp(s - m_new)
            l = l + jnp.sum(p, axis=1, keepdims=True)
            acc = acc + jnp.dot(p.astype(BF16), tile(v_refs[r], (0, 0), g), preferred_element_type=F32)
        m_ref[g], l_ref[g], acc_ref[g] = m_new, l, acc

    @pl.when(step == pl.num_programs(1) - 1)
    def _():
        for g in range(n_groups):
            o_ref[0, g] = acc_ref[g] / l_ref[g]


def fox_sample(page_table, q, cnk, k_new, v_new, cache_k, cache_v, bias_rows, layer, n_slots, t_new):
    bsz, n_pages = page_table.shape
    _, n_groups, grows, _ = q.shape
    _, _, page, n_heads, dh = cache_k.shape
    cols = page * HEAD_GROUP

    def page_idx(b, g, pt, r):
        return pt[b, n_pages - 1 - (g * n_slots + r)]

    def per_batch(shape):
        return pl.BlockSpec((1,) + shape, lambda b, g, pt: (b,) + (0,) * len(shape))

    in_specs = [per_batch((n_groups, grows, dh)),
                per_batch((n_groups, 1, 2 * t_new * HEAD_GROUP)),
                per_batch((2 * t_new, n_heads, dh)), per_batch((2 * t_new, n_heads, dh))]
    for _ in range(2):
        in_specs += [pl.BlockSpec((1, 1, page, n_heads, dh),
                                  functools.partial(lambda b, g, pt, r: (layer, page_idx(b, g, pt, r), 0, 0, 0), r=r))
                     for r in range(n_slots)]
    in_specs += [pl.BlockSpec((1, n_groups, 2, cols),
                              functools.partial(lambda b, g, pt, r: (page_idx(b, g, pt, r), 0, 0, 0), r=r))
                 for r in range(n_slots)]
    grid_spec = pltpu.PrefetchScalarGridSpec(
        num_scalar_prefetch=1,
        grid=(bsz, n_pages // n_slots),
        in_specs=in_specs,
        out_specs=per_batch((n_groups, grows, dh)),
        scratch_shapes=[pltpu.VMEM((n_groups, grows, 1), F32), pltpu.VMEM((n_groups, grows, 1), F32),
                        pltpu.VMEM((n_groups, grows, dh), F32), pltpu.VMEM((n_groups, 1, cols), F32)],
    )
    kern = functools.partial(_fox_sample_kernel, n_slots=n_slots, t_new=t_new)
    return pl.pallas_call(
        kern,
        grid_spec=grid_spec,
        out_shape=jax.ShapeDtypeStruct(q.shape, F32),
        compiler_params=_cparams("arbitrary", "arbitrary"),
    )(page_table, q, cnk, k_new, v_new,
      *([cache_k] * n_slots), *([cache_v] * n_slots), *([bias_rows] * n_slots))


def fox_sample_wrapper(page_table, zq, zk, zv, c_new, cache_k, cache_v, cache_lf, layer, bsz, t_new, n_slots,
                       bias_pages_per_step):
    _, pool, page, n_heads, dh = cache_k.shape
    n_groups = n_heads // HEAD_GROUP
    grows = HEAD_GROUP * t_new
    q = (zq * dh ** -0.5).reshape(bsz, t_new, n_heads, dh).transpose(0, 2, 1, 3).reshape(bsz, n_groups, grows, dh)
    c = c_new[:, :n_heads].reshape(bsz, t_new, n_groups, HEAD_GROUP)
    cnk = jnp.pad(c.transpose(0, 2, 1, 3), ((0, 0), (0, 0), (0, t_new), (0, 0)))
    cnk = cnk.reshape(bsz, n_groups, 1, 2 * t_new * HEAD_GROUP)
    pad = ((0, 0), (0, t_new), (0, 0), (0, 0))
    k_new = jnp.pad(zk.reshape(bsz, t_new, n_heads, dh), pad)
    v_new = jnp.pad(zv.reshape(bsz, t_new, n_heads, dh), pad)
    bias = fox_cache_bias(jnp.swapaxes(cache_lf, 2, 3), layer, bias_pages_per_step)
    bias_rows = bias.reshape(pool, n_groups, HEAD_GROUP, 2, page).transpose(0, 1, 3, 4, 2)
    bias_rows = bias_rows.reshape(pool, n_groups, 2, page * HEAD_GROUP)
    o = fox_sample(page_table, q, cnk, k_new, v_new, cache_k, cache_v, bias_rows, layer, n_slots, t_new)
    return o.reshape(bsz, n_heads, t_new, dh).transpose(0, 2, 1, 3).reshape(bsz * t_new, n_heads * dh)


def _layer_norm(h, g, b):
    mu = jnp.mean(h, axis=1, keepdims=True)
    d = h - mu
    var = jnp.mean(jnp.square(d), axis=1, keepdims=True)
    return d * lax.rsqrt(var + LN_EPS) * g + b


def _merge_kernel(ya_ref, yb_ref, wa_ref, wb_ref, ga_ref, gb_ref, bm_ref, o_ref):
    a = jnp.dot(ya_ref[...], wa_ref[...], preferred_element_type=F32)
    b = jnp.dot(yb_ref[...], wb_ref[...], preferred_element_type=F32)
    o = _sigmoid(ga_ref[...] + bm_ref[0:1]) * a + _sigmoid(gb_ref[...] + bm_ref[1:2]) * b
    o_ref[...] = o.astype(o_ref.dtype)


def merge_branches(ya, yb, wa, wb, z, ga_col, gb_col, b_merge, tm, tn):
    m, k = ya.shape
    n = wa.shape[1]
    return pl.pallas_call(
        _merge_kernel,
        grid=(m // tm, n // tn),
        in_specs=[pl.BlockSpec((tm, k), lambda i, j: (i, 0)),
                  pl.BlockSpec((tm, k), lambda i, j: (i, 0)),
                  pl.BlockSpec((k, tn), lambda i, j: (0, j)),
                  pl.BlockSpec((k, tn), lambda i, j: (0, j)),
                  pl.BlockSpec((tm, tn), lambda i, j: (i, ga_col + j)),
                  pl.BlockSpec((tm, tn), lambda i, j: (i, gb_col + j)),
                  pl.BlockSpec((2, tn), lambda i, j: (0, j))],
        out_specs=pl.BlockSpec((tm, tn), lambda i, j: (i, j)),
        out_shape=jax.ShapeDtypeStruct((m, n), BF16),
        compiler_params=_cparams("parallel", "parallel"),
    )(ya, yb, wa, wb, z, z, b_merge)


def _out_ln_kernel(m_ref, w_ref, x_ref, g_ref, b_ref, u_ref, ub_ref, *, alpha):
    mix = jnp.dot(m_ref[...], w_ref[...], preferred_element_type=F32)
    u = _layer_norm(alpha * x_ref[...] + mix, g_ref[...], b_ref[...])
    u_ref[...] = u
    ub_ref[...] = u.astype(BF16)


def out_proj_ln(merged, w_out, x, g, b, alpha, tm):
    m, k = merged.shape
    n = w_out.shape[1]
    row = lambda i: (i, 0)
    fixed = lambda i: (0, 0)
    return pl.pallas_call(
        functools.partial(_out_ln_kernel, alpha=alpha),
        grid=(m // tm,),
        in_specs=[pl.BlockSpec((tm, k), row), pl.BlockSpec((k, n), fixed), pl.BlockSpec((tm, n), row),
                  pl.BlockSpec((1, n), fixed), pl.BlockSpec((1, n), fixed)],
        out_specs=[pl.BlockSpec((tm, n), row), pl.BlockSpec((tm, n), row)],
        out_shape=[jax.ShapeDtypeStruct((m, n), F32), jax.ShapeDtypeStruct((m, n), BF16)],
        compiler_params=_cparams("parallel"),
    )(merged, w_out, x, g, b)


def _ple_kernel(u_ref, wg_ref, pe_ref, wp_ref, o_ref):
    gate = _sigmoid(jnp.dot(u_ref[...], wg_ref[...], preferred_element_type=F32))
    proj = jnp.dot(pe_ref[...], wp_ref[...], preferred_element_type=F32)
    o_ref[...] = (gate * proj).astype(o_ref.dtype)


def ple_term(ub, w_gate, pe, w_proj, tm, tn):
    m, k = ub.shape
    n = w_gate.shape[1]
    kp = pe.shape[1]
    return pl.pallas_call(
        _ple_kernel,
        grid=(m // tm, n // tn),
        in_specs=[pl.BlockSpec((tm, k), lambda i, j: (i, 0)),
                  pl.BlockSpec((k, tn), lambda i, j: (0, j)),
                  pl.BlockSpec((tm, kp), lambda i, j: (i, 0)),
                  pl.BlockSpec((kp, tn), lambda i, j: (0, j))],
        out_specs=pl.BlockSpec((tm, tn), lambda i, j: (i, j)),
        out_shape=jax.ShapeDtypeStruct((m, n), BF16),
        compiler_params=_cparams("parallel", "parallel"),
    )(ub, w_gate, pe, w_proj)


def _router_kernel(u_ref, wrt_ref, br_ref, comb_ref, sel_ref):
    scores = _sigmoid(_dot_nt(wrt_ref[...], u_ref[...], precision=HI))
    sel = scores + br_ref[...]
    n_exp, tm = sel.shape
    gsize = n_exp // N_GROUPS
    groups = [sel[g * gsize:(g + 1) * gsize] for g in range(N_GROUPS)]
    rows = []
    for x in groups:
        m1 = jnp.max(x, axis=0, keepdims=True)
        dup = jnp.sum(jnp.where(x == m1, 1.0, 0.0), axis=0, keepdims=True)
        m2 = jnp.max(jnp.where(x < m1, x, -jnp.inf), axis=0, keepdims=True)
        rows.append(m1 + jnp.where(dup >= 2.0, m1, m2))
    gscore = jnp.concatenate(rows, axis=0)
    gidx = lax.broadcasted_iota(jnp.int32, gscore.shape, 0)
    grank = jnp.zeros(gscore.shape, F32)
    for g in range(N_GROUPS):
        o = gscore[g:g + 1]
        grank = grank + jnp.where(o > gscore, 1.0, jnp.where(o == gscore, jnp.where(g < gidx, 1.0, 0.0), 0.0))
    masked = jnp.concatenate(
        [jnp.where(grank[g:g + 1] < TOPK_GROUPS, groups[g], NEG_INF) for g in range(N_GROUPS)], axis=0)
    sel_ref[...] = masked
    eidx = lax.broadcasted_iota(jnp.int32, masked.shape, 0)

    def count_better(e, rank):
        o = sel_ref[pl.ds(e, 1), :]
        return rank + jnp.where(o > masked, 1.0, jnp.where(o == masked, jnp.where(e < eidx, 1.0, 0.0), 0.0))

    rank = lax.fori_loop(0, n_exp, count_better, jnp.zeros(masked.shape, F32))
    w = jnp.where(rank < TOP_K, scores, 0.0)
    w = w / jnp.sum(w, axis=0, keepdims=True) * ROUTE_SCALE
    pad = jnp.zeros((comb_ref.shape[1] - n_exp, tm), F32)
    comb_ref[...] = jnp.concatenate([w, pad], axis=0).T


def router(u, w_router_t, b_router, tm):
    m, k = u.shape
    n_exp = w_router_t.shape[0]
    return pl.pallas_call(
        _router_kernel,
        grid=(m // tm,),
        in_specs=[pl.BlockSpec((tm, k), lambda i: (i, 0)),
                  pl.BlockSpec((n_exp, k), lambda i: (0, 0)),
                  pl.BlockSpec((n_exp, 1), lambda i: (0, 0))],
        out_specs=pl.BlockSpec((tm, LANES), lambda i: (i, 0)),
        out_shape=jax.ShapeDtypeStruct((m, LANES), F32),
        scratch_shapes=[pltpu.VMEM((n_exp, tm), F32)],
        compiler_params=_cparams("parallel"),
    )(u, w_router_t, b_router)


def _moe_kernel(u_ref, comb_ref, wg_ref, wu_ref, wd_ref, o_ref):
    e = pl.program_id(1)

    @pl.when(e == 0)
    def _():
        o_ref[...] = jnp.zeros_like(o_ref)

    u = u_ref[...]
    hg = jnp.dot(u, wg_ref[0].astype(BF16), preferred_element_type=F32)
    hu = jnp.dot(u, wu_ref[0].astype(BF16), preferred_element_type=F32)
    comb = comb_ref[...]
    lane = lax.broadcasted_iota(jnp.int32, comb.shape, 1)
    c = jnp.sum(jnp.where(lane == e, comb, 0.0), axis=1, keepdims=True)
    h = (_silu(hg) * hu * c).astype(BF16)
    o_ref[...] += jnp.dot(h, wd_ref[0].astype(BF16), preferred_element_type=F32)


def moe_routed(ub, comb, w_gate, w_up, w_down, tm):
    m, k = ub.shape
    n_exp, _, f = w_gate.shape
    return pl.pallas_call(
        _moe_kernel,
        grid=(m // tm, n_exp),
        in_specs=[pl.BlockSpec((tm, k), lambda i, e: (i, 0)),
                  pl.BlockSpec((tm, LANES), lambda i, e: (i, 0)),
                  pl.BlockSpec((1, k, f), lambda i, e: (e, 0, 0)),
                  pl.BlockSpec((1, k, f), lambda i, e: (e, 0, 0)),
                  pl.BlockSpec((1, f, k), lambda i, e: (e, 0, 0))],
        out_specs=pl.BlockSpec((tm, k), lambda i, e: (i, 0)),
        out_shape=jax.ShapeDtypeStruct((m, k), F32),
        compiler_params=_cparams("parallel", "arbitrary"),
    )(ub, comb, w_gate, w_up, w_down)


def _final_kernel(u_ref, ub_ref, routed_ref, ple_ref, wsg_ref, wsu_ref, wsd_ref, g_ref, b_ref, y_ref, *, alpha):
    ub = ub_ref[...]
    hs = _silu(jnp.dot(ub, wsg_ref[...], preferred_element_type=F32)) * jnp.dot(
        ub, wsu_ref[...], preferred_element_type=F32)
    shared = jnp.dot(hs.astype(BF16), wsd_ref[...], preferred_element_type=F32)
    h = alpha * u_ref[...] + (routed_ref[...] + shared) + ple_ref[...].astype(F32)
    y_ref[...] = _layer_norm(h, g_ref[...], b_ref[...])


def final_ln(u, ub, routed, ple, wsg, wsu, wsd, g, b, alpha, tm):
    m, n = u.shape
    f = wsg.shape[1]
    row = lambda i: (i, 0)
    fixed = lambda i: (0, 0)
    return pl.pallas_call(
        functools.partial(_final_kernel, alpha=alpha),
        grid=(m // tm,),
        in_specs=[pl.BlockSpec((tm, n), row), pl.BlockSpec((tm, n), row), pl.BlockSpec((tm, n), row),
                  pl.BlockSpec((tm, n), row), pl.BlockSpec((n, f), fixed), pl.BlockSpec((n, f), fixed),
                  pl.BlockSpec((f, n), fixed), pl.BlockSpec((1, n), fixed), pl.BlockSpec((1, n), fixed)],
        out_specs=pl.BlockSpec((tm, n), row),
        out_shape=jax.ShapeDtypeStruct((m, n), F32),
        compiler_params=_cparams("parallel"),
    )(u, ub, routed, ple, wsg, wsu, wsd, g, b)


TM_MATMUL = 832
TM_NORM = 416
TM_ROUTER = 640
TN_IN = 1024
TN_MERGE = 512
TN_PLE = 1024
HGRN_ROWS = 128
FOX_GATE_ROWS = 256
FOX_BLK = 1024
FOX_SLOTS = 8
FOX_BIAS_PAGES = 64


def kernel(x_prompt, x_sample, p_prompt, p_sample, cache_k, cache_v, cache_lf, state_hgrn, page_table, w_in, b_fox_f, b_merge, lb_param, hgrn_norm_g, w_proj_a, w_proj_b, w_out, ln1_g, ln1_b, ln2_g, ln2_b, w_router, b_router, w_exp_gate, w_exp_up, w_exp_down, w_sh_gate, w_sh_up, w_sh_down, w_ple_proj, w_ple_gate):
    depth, d_model, _ = w_in.shape
    assert depth == 1 and x_prompt.shape[0] == 1, "single layer, single prompt sequence"
    t_p = x_prompt.shape[1]
    bsz, t_new, _ = x_sample.shape
    t_s = bsz * t_new
    n_ha, dk_a, dv_a = state_hgrn.shape[2:]
    _, pool, page, n_hb, dh_b = cache_k.shape
    assert dk_a == LANES and dv_a == LANES and dh_b == LANES and n_hb % 2 == 0
    wa_cols = n_ha * LANES
    wb_cols = n_hb * LANES
    c_qb = 4 * wa_cols
    c_fb = c_qb + 3 * wb_cols
    alpha = (2.0 * depth) ** 0.25
    l = 0

    x_all = jnp.concatenate([x_prompt.reshape(t_p, d_model), x_sample.reshape(t_s, d_model)], axis=0)
    wt_in = jnp.swapaxes(w_in, 1, 2)
    wt_fb = jnp.pad(wt_in[l:l + 1, c_fb:c_fb + n_hb], ((0, 0), (0, LANES - n_hb), (0, 0)))
    z = in_proj(x_all, wt_in, l, 0, c_fb, TM_MATMUL, TN_IN)
    zg = in_proj(x_all, wt_in, l, c_fb + n_hb, 2 * d_model, TM_MATMUL, TN_IN)
    zf = in_proj(x_all, wt_fb, 0, 0, LANES, TM_MATMUL, LANES)

    ng = hgrn_norm_g[l][None]
    ya_p, s_p = hgrn_prompt(z, lb_param, ng, t_p, n_ha, HGRN_ROWS)
    zs = jnp.pad(z[t_p:, :c_qb].reshape(bsz, t_new, c_qb), ((0, 0), (0, HGRN_BLK - t_new), (0, 0)))
    ya_s, s_s = hgrn_sample(zs.reshape(bsz * HGRN_BLK, c_qb), lb_param, ng, state_hgrn[l], n_ha, t_new)
    ya_s = ya_s.reshape(bsz, HGRN_BLK, wa_cols)[:, :t_new].reshape(t_s, wa_cols)

    fbias = jnp.pad(b_fox_f[l], (0, LANES - n_hb))[None]
    lf_p, c_p = fox_gate_prompt(zf, fbias, t_p, FOX_GATE_ROWS)
    lf_s, c_s = fox_gate_sample(zf[t_p:], fbias, t_new)
    q_col = c_qb // LANES
    yb_p = fox_prompt(z, c_p, t_p, n_hb, q_col, q_col + n_hb, q_col + 2 * n_hb, FOX_BLK)
    z_s = z[t_p:]
    k_s = z_s[:, c_qb + wb_cols:c_qb + 2 * wb_cols]
    v_s = z_s[:, c_qb + 2 * wb_cols:c_fb]
    yb_s = fox_sample_wrapper(page_table, z_s[:, c_qb:c_qb + wb_cols], k_s, v_s, c_s,
                              cache_k, cache_v, cache_lf, l, bsz, t_new, FOX_SLOTS, FOX_BIAS_PAGES)

    ya = jnp.concatenate([ya_p, ya_s], axis=0)
    yb = jnp.concatenate([yb_p, yb_s.astype(BF16)], axis=0)
    merged = merge_branches(ya, yb, w_proj_a[l].astype(BF16), w_proj_b[l].astype(BF16), zg,
                            0, d_model // TN_MERGE, b_merge[l], TM_MATMUL, TN_MERGE)
    u, ub = out_proj_ln(merged, w_out[l].astype(BF16), x_all, ln1_g[l][None], ln1_b[l][None], alpha, TM_NORM)

    pe = jnp.concatenate([p_prompt[l].reshape(t_p, -1), p_sample[l].reshape(t_s, -1)], axis=0).astype(BF16)
    ple = ple_term(ub, w_ple_gate[l].astype(BF16), pe, w_ple_proj[l].astype(BF16), TM_MATMUL, TN_PLE)
    comb = router(u, w_router[l].T, b_router[l][:, None], TM_ROUTER)
    routed = moe_routed(ub, comb, w_exp_gate[l], w_exp_up[l], w_exp_down[l], TM_MATMUL)
    y = final_ln(u, ub, routed, ple, w_sh_gate[l].astype(BF16), w_sh_up[l].astype(BF16),
                 w_sh_down[l].astype(BF16), ln2_g[l][None], ln2_b[l][None], alpha, TM_NORM)

    k_p = z[:t_p, c_qb + wb_cols:c_qb + 2 * wb_cols]
    v_p = z[:t_p, c_qb + 2 * wb_cols:c_fb]
    return (y[:t_p].reshape(1, t_p, d_model),
            y[t_p:].reshape(bsz, t_new, d_model),
            k_p.reshape(1, 1, t_p, n_hb, dh_b),
            v_p.reshape(1, 1, t_p, n_hb, dh_b),
            lf_p[:, :n_hb].reshape(1, 1, t_p, n_hb),
            s_p[None, None],
            k_s.reshape(1, bsz, t_new, n_hb, dh_b),
            v_s.reshape(1, bsz, t_new, n_hb, dh_b),
            lf_s[:, :n_hb].reshape(1, bsz, t_new, n_hb),
            s_s[None])
```

```python
import functools

import jax
import jax.numpy as jnp
from jax import lax
from jax.experimental import pallas as pl
from jax.experimental.pallas import tpu as pltpu

F32 = jnp.float32
BF16 = jnp.bfloat16

LANES = 128
VMEM_LIMIT = 56 * 1024 * 1024

LN_EPS = 1e-5
NEG_INF = -1e30
ROUTE_SCALE = 2.5
N_GROUPS = 8
TOPK_GROUPS = 4
TOP_K = 8
HGRN_BLK = 16
HGRN_HEADS_PER_ITER = 4

HI = lax.Precision.HIGHEST


def _cparams(*sem):
    return pltpu.CompilerParams(dimension_semantics=sem, vmem_limit_bytes=VMEM_LIMIT)


def _sigmoid(x):
    return 1.0 / (1.0 + jnp.exp(-x))


def _silu(x):
    return x * _sigmoid(x)


def _log_sigmoid(x):
    return jnp.minimum(x, 0.0) - jnp.log1p(jnp.exp(-jnp.abs(x)))


def _dot_nt(a, b, precision=None):
    return lax.dot_general(a, b, (((1,), (1,)), ((), ())), preferred_element_type=F32, precision=precision)


def _dot_tn(a, b, precision=None):
    return lax.dot_general(a, b, (((0,), (0,)), ((), ())), preferred_element_type=F32, precision=precision)


def _split3(x):
    hi = x.astype(BF16)
    rest = x - hi.astype(F32)
    mid = rest.astype(BF16)
    return hi, mid, (rest - mid.astype(F32)).astype(BF16)


def _dot_mask(mask, x):
    mb = mask.astype(BF16)
    hi, mid, lo = _split3(x)
    return (jnp.dot(mb, hi, preferred_element_type=F32) + jnp.dot(mb, mid, preferred_element_type=F32)
            + jnp.dot(mb, lo, preferred_element_type=F32))


def _dot_mask_rhs(x, mask):
    mb = mask.astype(BF16)
    hi, mid, lo = _split3(x)
    return (jnp.dot(hi, mb, preferred_element_type=F32) + jnp.dot(mid, mb, preferred_element_type=F32)
            + jnp.dot(lo, mb, preferred_element_type=F32))


def _in_proj_kernel(x_ref, wt_ref, o_ref, wb_ref):
    @pl.when(pl.program_id(1) == 0)
    def _():
        wb_ref[...] = wt_ref[0].astype(BF16)

    o_ref[...] = _dot_nt(x_ref[...], wb_ref[...])


def in_proj(xb, wt, layer, n_cols, tm, tn):
    m, k = xb.shape
    return pl.pallas_call(
        _in_proj_kernel,
        grid=(n_cols // tn, m // tm),
        in_specs=[pl.BlockSpec((tm, k), lambda j, i: (i, 0)),
                  pl.BlockSpec((1, tn, k), lambda j, i: (layer, j, 0))],
        out_specs=pl.BlockSpec((tm, tn), lambda j, i: (i, j)),
        out_shape=jax.ShapeDtypeStruct((m, n_cols), F32),
        scratch_shapes=[pltpu.VMEM((tn, k), BF16)],
        compiler_params=_cparams("arbitrary", "arbitrary"),
    )(xb, wt)


def _hgrn_head(zq, zf, zi, zg, lb, ng, st, mats, n_keys, dk):
    rows = zq.shape[0]
    q = _silu(zq) * dk ** -0.5
    logf = jnp.log(lb + (1.0 - lb) * _sigmoid(zf))
    k = (1.0 - lb) * _sigmoid(-zf)
    if n_keys < HGRN_BLK:
        valid = lax.broadcasted_iota(jnp.int32, (rows, 1), 0) % HGRN_BLK < n_keys
        logf = jnp.where(valid, logf, 0.0)
        k = jnp.where(valid, k, 0.0)
    both = jnp.dot(mats, logf, preferred_element_type=F32, precision=HI)
    b, blast = both[:rows], both[rows:]
    qt = (q * jnp.exp(b)).astype(BF16)
    kt = (k * jnp.exp(blast - b)).astype(BF16)
    vb = zi.astype(BF16)
    trow = lax.broadcasted_iota(jnp.int32, (HGRN_BLK, 1), 0)
    blocks = [slice(i * HGRN_BLK, (i + 1) * HGRN_BLK) for i in range(rows // HGRN_BLK)]
    incs = [_dot_tn(vb[sl], kt[sl]) for sl in blocks]
    states = []
    for sl, inc in zip(blocks, incs):
        states.append(st)
        st = st * jnp.exp(blast[sl.start:sl.start + 1]) + inc
    outs = [_dot_nt(qt[sl], s_in.astype(BF16)) for sl, s_in in zip(blocks, states)]
    for i, sl in enumerate(blocks):
        qb, kb, bb, vv = q[sl], k[sl], b[sl], zi[sl]
        o_blk = outs[i]
        for s in range(n_keys):
            e = jnp.exp(jnp.minimum(bb - bb[s:s + 1], 0.0))
            col = jnp.sum(qb * kb[s:s + 1] * e, axis=1, keepdims=True)
            o_blk = o_blk + jnp.where(trow >= s, col, 0.0) * vv[s:s + 1]
        outs[i] = o_blk
    o = jnp.concatenate(outs, axis=0)
    o = o * lax.rsqrt(jnp.mean(jnp.square(o), axis=1, keepdims=True) + LN_EPS)
    return o * ng * _silu(zg), st


def _block_mats(rows):
    r = lax.broadcasted_iota(jnp.int32, (rows, rows), 0)
    c = lax.broadcasted_iota(jnp.int32, (rows, rows), 1)
    same = (r // HGRN_BLK) == (c // HGRN_BLK)
    lmat = jnp.where(same & (c <= r), 1.0, 0.0).astype(F32)
    bmat = jnp.where(same, 1.0, 0.0).astype(F32)
    return jnp.concatenate([lmat, bmat], axis=0)


def _lower_bound(lbp_ref, sl):
    p = lbp_ref[:, sl]
    m = jnp.max(p, axis=0, keepdims=True)
    e = jnp.exp(p - m)
    return e[0:1] / jnp.sum(e, axis=0, keepdims=True)


def _hgrn_prompt_kernel(zq_ref, zf_ref, zi_ref, zg_ref, lbp_ref, ng_ref, y_ref, s_ref, st_ref, *, n_heads, dk):
    step = pl.program_id(0)

    @pl.when(step == 0)
    def _():
        st_ref[...] = jnp.zeros_like(st_ref)

    mats = _block_mats(zq_ref.shape[0])

    def head_group(i, carry):
        heads = [i * HGRN_HEADS_PER_ITER + j for j in range(HGRN_HEADS_PER_ITER)]
        lanes = [pl.ds(pl.multiple_of(h * LANES, LANES), LANES) for h in heads]
        args = [(zq_ref[:, sl], zf_ref[:, sl], zi_ref[:, sl], zg_ref[:, sl],
                 _lower_bound(lbp_ref, sl), ng_ref[:, sl], st_ref[h]) for h, sl in zip(heads, lanes)]
        outs = [_hgrn_head(*a, mats, HGRN_BLK, dk) for a in args]
        for h, sl, (y, st) in zip(heads, lanes, outs):
            y_ref[:, sl] = y.astype(y_ref.dtype)
            st_ref[h] = st
        return carry

    lax.fori_loop(0, n_heads // HGRN_HEADS_PER_ITER, head_group, 0)

    @pl.when(step == pl.num_programs(0) - 1)
    def _():
        for h in range(n_heads):
            s_ref[h] = st_ref[h].T


def hgrn_prompt(z, lb_param, norm_g, t_len, n_heads, rows):
    width = n_heads * LANES
    kern = functools.partial(_hgrn_prompt_kernel, n_heads=n_heads, dk=LANES)
    return pl.pallas_call(
        kern,
        grid=(t_len // rows,),
        in_specs=[pl.BlockSpec((rows, width), lambda i: (i, 0)),
                  pl.BlockSpec((rows, width), lambda i: (i, 1)),
                  pl.BlockSpec((rows, width), lambda i: (i, 2)),
                  pl.BlockSpec((rows, width), lambda i: (i, 3)),
                  pl.BlockSpec((2, width), lambda i: (0, 0)),
                  pl.BlockSpec((1, width), lambda i: (0, 0))],
        out_specs=[pl.BlockSpec((rows, width), lambda i: (i, 0)),
                   pl.BlockSpec((n_heads, LANES, LANES), lambda i: (0, 0, 0))],
        out_shape=[jax.ShapeDtypeStruct((t_len, width), BF16),
                   jax.ShapeDtypeStruct((n_heads, LANES, LANES), F32)],
        scratch_shapes=[pltpu.VMEM((n_heads, LANES, LANES), F32)],
        compiler_params=_cparams("arbitrary"),
    )(z, z, z, z, lb_param, norm_g)


def _hgrn_sample_kernel(zq_ref, zf_ref, zi_ref, zg_ref, lbp_ref, ng_ref, s0_ref, y_ref, s_ref, *, n_heads, dk, t_new):
    rows = zq_ref.shape[0]
    mats = _block_mats(rows)

    def head_group(i, carry):
        heads = [i * HGRN_HEADS_PER_ITER + j for j in range(HGRN_HEADS_PER_ITER)]
        lanes = [pl.ds(pl.multiple_of(h * LANES, LANES), LANES) for h in heads]
        args = [(zq_ref[:, sl], zf_ref[:, sl], zi_ref[:, sl], zg_ref[:, sl],
                 _lower_bound(lbp_ref, sl), ng_ref[:, sl], s0_ref[0, h].T) for h, sl in zip(heads, lanes)]
        outs = [_hgrn_head(*a, mats, t_new, dk) for a in args]
        for h, sl, (y, st) in zip(heads, lanes, outs):
            y_ref[:, sl] = y.astype(y_ref.dtype)
            s_ref[0, h] = st.T
        return carry

    lax.fori_loop(0, n_heads // HGRN_HEADS_PER_ITER, head_group, 0)


def hgrn_sample(zs, lb_param, norm_g, s0, n_heads, t_new):
    bsz = s0.shape[0]
    width = n_heads * LANES
    kern = functools.partial(_hgrn_sample_kernel, n_heads=n_heads, dk=LANES, t_new=t_new)
    return pl.pallas_call(
        kern,
        grid=(bsz,),
        in_specs=[pl.BlockSpec((HGRN_BLK, width), lambda b: (b, 0)),
                  pl.BlockSpec((HGRN_BLK, width), lambda b: (b, 1)),
                  pl.BlockSpec((HGRN_BLK, width), lambda b: (b, 2)),
                  pl.BlockSpec((HGRN_BLK, width), lambda b: (b, 3)),
                  pl.BlockSpec((2, width), lambda b: (0, 0)),
                  pl.BlockSpec((1, width), lambda b: (0, 0)),
                  pl.BlockSpec((1, n_heads, LANES, LANES), lambda b: (b, 0, 0, 0))],
        out_specs=[pl.BlockSpec((HGRN_BLK, width), lambda b: (b, 0)),
                   pl.BlockSpec((1, n_heads, LANES, LANES), lambda b: (b, 0, 0, 0))],
        out_shape=[jax.ShapeDtypeStruct((bsz * HGRN_BLK, width), BF16),
                   jax.ShapeDtypeStruct(s0.shape, F32)],
        compiler_params=_cparams("parallel"),
    )(zs, zs, zs, zs, lb_param, norm_g, s0)


def _fox_gate_prompt_kernel(zf_ref, bias_ref, lf_ref, ct_ref, carry_ref):
    @pl.when(pl.program_id(0) == 0)
    def _():
        carry_ref[...] = jnp.zeros_like(carry_ref)

    lf = _log_sigmoid(zf_ref[...] + bias_ref[...])
    lf_ref[...] = lf
    rows = lf.shape[0]
    r = lax.broadcasted_iota(jnp.int32, (rows, rows), 0)
    c = lax.broadcasted_iota(jnp.int32, (rows, rows), 1)
    lower = jnp.where(c <= r, 1.0, 0.0).astype(BF16)
    cs = _dot_mask(lower, lf) + carry_ref[...]
    ct_ref[...] = cs
    carry_ref[...] = cs[rows - 1:rows]


def fox_gate_prompt(zf, bias, t_len, rows):
    return pl.pallas_call(
        _fox_gate_prompt_kernel,
        grid=(t_len // rows,),
        in_specs=[pl.BlockSpec((rows, LANES), lambda i: (i, 0)),
                  pl.BlockSpec((1, LANES), lambda i: (0, 0))],
        out_specs=[pl.BlockSpec((rows, LANES), lambda i: (i, 0)),
                   pl.BlockSpec((rows, LANES), lambda i: (i, 0))],
        out_shape=[jax.ShapeDtypeStruct((t_len, LANES), F32),
                   jax.ShapeDtypeStruct((t_len, LANES), F32)],
        scratch_shapes=[pltpu.VMEM((1, LANES), F32)],
        compiler_params=_cparams("arbitrary"),
    )(zf, bias)


def _fox_gate_sample_kernel(zf_ref, bias_ref, lf_ref, c_ref, *, t_new):
    lf = _log_sigmoid(zf_ref[...] + bias_ref[...])
    lf_ref[...] = lf
    rows = lf.shape[0]
    r = lax.broadcasted_iota(jnp.int32, (rows, rows), 0)
    c = lax.broadcasted_iota(jnp.int32, (rows, rows), 1)
    tri = jnp.where(((r // t_new) == (c // t_new)) & (c <= r), 1.0, 0.0).astype(BF16)
    c_ref[...] = _dot_mask(tri, lf)


def fox_gate_sample(zf, bias, t_new):
    rows = zf.shape[0]
    return pl.pallas_call(
        functools.partial(_fox_gate_sample_kernel, t_new=t_new),
        grid=(1,),
        in_specs=[pl.BlockSpec((rows, LANES), lambda i: (0, 0)),
                  pl.BlockSpec((1, LANES), lambda i: (0, 0))],
        out_specs=[pl.BlockSpec((rows, LANES), lambda i: (0, 0)),
                   pl.BlockSpec((rows, LANES), lambda i: (0, 0))],
        out_shape=[jax.ShapeDtypeStruct((rows, LANES), F32),
                   jax.ShapeDtypeStruct((rows, LANES), F32)],
        compiler_params=_cparams("arbitrary"),
    )(zf, bias)


LOG2E = 1.4426950408889634
FOX_ROW_CHUNKS = 4


def _fox_prompt_kernel(q_ref, k_ref, v_ref, c_ref, o_ref, kb_ref, vb_ref, *, blk, scale):
    h = pl.program_id(0)
    i = pl.program_id(1)
    t_len = k_ref.shape[0]

    @pl.when(i == 0)
    def _():
        lane = lax.broadcasted_iota(jnp.int32, (t_len, LANES), 1)
        beta = -LOG2E * jnp.sum(jnp.where(lane == h, c_ref[...], 0.0), axis=1, keepdims=True)
        hi = beta.astype(BF16).astype(F32)
        mid = (beta - hi).astype(BF16).astype(F32)
        lo = beta - hi - mid
        aug = jnp.where(lane == 0, hi, jnp.where(lane == 1, mid, jnp.where(lane == 2, lo, 0.0)))
        kb_ref[:, :LANES] = k_ref[...].astype(BF16)
        kb_ref[:, LANES:] = aug.astype(BF16)
        vb_ref[:, :LANES] = v_ref[...].astype(BF16)
        vb_ref[:, LANES:] = jnp.where(lane == 0, 1.0, 0.0).astype(BF16)

    qlane = lax.broadcasted_iota(jnp.int32, (blk, LANES), 1)
    q = jnp.concatenate([(q_ref[...] * (scale * LOG2E)).astype(BF16),
                         jnp.where(qlane < 3, 1.0, 0.0).astype(BF16)], axis=1)

    sub = blk // FOX_ROW_CHUNKS
    qs = [q[r * sub:(r + 1) * sub] for r in range(FOX_ROW_CHUNKS)]
    row = lax.broadcasted_iota(jnp.int32, (sub, blk), 0)
    col = lax.broadcasted_iota(jnp.int32, (sub, blk), 1)

    def update(j, state, diagonal):
        off = pl.multiple_of(j * blk, blk)
        kj = kb_ref[pl.ds(off, blk), :]
        vj = vb_ref[pl.ds(off, blk), :]
        scored = [_dot_nt(qr, kj) for qr in qs]
        out = []
        for r, (s, (m, acc)) in enumerate(zip(scored, state)):
            if diagonal:
                s = jnp.where(col <= row + r * sub, s, NEG_INF)
            m_new = jnp.maximum(m, jnp.max(s, axis=1, keepdims=True))
            p = jnp.exp2(s - m_new).astype(BF16)
            out.append((m_new, jnp.exp2(m - m_new) * acc + jnp.dot(p, vj, preferred_element_type=F32)))
        return tuple(out)

    init = tuple((jnp.full((sub, 1), NEG_INF, F32), jnp.zeros((sub, 2 * LANES), F32))
                 for _ in range(FOX_ROW_CHUNKS))
    state = lax.fori_loop(0, i, lambda j, st: update(j, st, False), init)
    state = update(i, state, True)
    for r, (_, acc) in enumerate(state):
        o_ref[r * sub:(r + 1) * sub, :] = (acc[:, :LANES] / acc[:, LANES:LANES + 1]).astype(o_ref.dtype)


def fox_prompt(z, c, t_len, n_heads, q_col, k_col, v_col, blk):
    kern = functools.partial(_fox_prompt_kernel, blk=blk, scale=LANES ** -0.5)
    return pl.pallas_call(
        kern,
        grid=(n_heads, t_len // blk),
        in_specs=[pl.BlockSpec((blk, LANES), lambda h, i: (i, q_col + h)),
                  pl.BlockSpec((t_len, LANES), lambda h, i: (0, k_col + h)),
                  pl.BlockSpec((t_len, LANES), lambda h, i: (0, v_col + h)),
                  pl.BlockSpec((t_len, LANES), lambda h, i: (0, 0))],
        out_specs=pl.BlockSpec((blk, LANES), lambda h, i: (i, h)),
        out_shape=jax.ShapeDtypeStruct((t_len, n_heads * LANES), BF16),
        scratch_shapes=[pltpu.VMEM((t_len, 2 * LANES), BF16), pltpu.VMEM((t_len, 2 * LANES), BF16)],
        compiler_params=_cparams("arbitrary", "arbitrary"),
    )(z, z, z, c)


HEAD_GROUP = 8


def _cache_bias_kernel(lf_ref, o_ref):
    _, n_pages, n_heads, page = lf_ref.shape
    ks = lax.broadcasted_iota(jnp.int32, (page, 2 * page), 0)
    kj = lax.broadcasted_iota(jnp.int32, (page, 2 * page), 1)
    later = jnp.where(ks > kj, 1.0, jnp.where(kj >= page, 1.0, 0.0)).astype(BF16)
    lf = lf_ref[0].reshape(n_pages * n_heads, page)
    o_ref[...] = _dot_mask_rhs(lf, later).reshape(n_pages, n_heads, 2 * page)


def fox_cache_bias(cache_lf_t, layer, pages_per_step):
    depth, pool, n_heads, page = cache_lf_t.shape
    return pl.pallas_call(
        _cache_bias_kernel,
        grid=(pool // pages_per_step,),
        in_specs=[pl.BlockSpec((1, pages_per_step, n_heads, page), lambda i: (layer, i, 0, 0))],
        out_specs=pl.BlockSpec((pages_per_step, n_heads, 2 * page), lambda i: (i, 0, 0)),
        out_shape=jax.ShapeDtypeStruct((pool, n_heads, 2 * page), F32),
        compiler_params=_cparams("parallel"),
    )(cache_lf_t)


def _fox_sample_kernel(pt_ref, q_ref, cnk_ref, kn_ref, vn_ref, *rest, n_slots, t_new):
    k_refs, v_refs, b_refs = rest[:n_slots], rest[n_slots:2 * n_slots], rest[2 * n_slots:3 * n_slots]
    o_ref = rest[3 * n_slots]
    m_ref, l_ref, acc_ref, carry_ref = rest[3 * n_slots + 1:]
    step = pl.program_id(1)
    n_groups = q_ref.shape[1]
    grows = q_ref.shape[2]
    page = k_refs[0].shape[2]

    def own_head(n_keys):
        r = lax.broadcasted_iota(jnp.int32, (grows, n_keys * HEAD_GROUP), 0)
        c = lax.broadcasted_iota(jnp.int32, (grows, n_keys * HEAD_GROUP), 1)
        return r, c, (c % HEAD_GROUP) == (r // t_new)

    def tile(ref, lead, g):
        x = ref[lead + (slice(None), slice(g * HEAD_GROUP, (g + 1) * HEAD_GROUP), slice(None))]
        return x.reshape(x.shape[0] * HEAD_GROUP, x.shape[2]).astype(BF16)

    @pl.when(step == 0)
    def _():
        r, c, same = own_head(2 * t_new)
        visible = same & ((c // HEAD_GROUP) <= (r % t_new))
        for g in range(n_groups):
            s = _dot_nt(q_ref[0, g].astype(BF16), tile(kn_ref, (0,), g))
            s = jnp.where(visible, s - cnk_ref[0, g], NEG_INF)
            m0 = jnp.max(s, axis=1, keepdims=True)
            p0 = jnp.exp(s - m0)
            m_ref[g] = m0
            l_ref[g] = jnp.sum(p0, axis=1, keepdims=True)
            acc_ref[g] = jnp.dot(p0.astype(BF16), tile(vn_ref, (0,), g), preferred_element_type=F32)
        carry_ref[...] = jnp.zeros_like(carry_ref)

    _, _, same = own_head(page)
    other_head = jnp.where(same, 0.0, NEG_INF)
    for g in range(n_groups):
        qg = q_ref[0, g].astype(BF16)
        carry = carry_ref[g]
        parts = []
        for r in range(n_slots):
            bias = other_head + (carry + b_refs[r][0, g, 0:1, :])
            parts.append(_dot_nt(qg, tile(k_refs[r], (0, 0), g)) + bias)
            carry = carry + b_refs[r][0, g, 1:2, :]
        carry_ref[g] = carry
        m = m_ref[g]
        m_new = m
        for s in parts:
            m_new = jnp.maximum(m_new, jnp.max(s, axis=1, keepdims=True))
        alpha = jnp.exp(m - m_new)
        l = alpha * l_ref[g]
        acc = alpha * acc_ref[g]
        for r, s in enumerate(parts):
            p = jnp.exp(s - m_new)
            l = l + jnp.sum(p, axis=1, keepdims=True)
            acc = acc + jnp.dot(p.astype(BF16), tile(v_refs[r], (0, 0), g), preferred_element_type=F32)
        m_ref[g], l_ref[g], acc_ref[g] = m_new, l, acc

    @pl.when(step == pl.num_programs(1) - 1)
    def _():
        for g in range(n_groups):
            o_ref[0, g] = acc_ref[g] / l_ref[g]


def fox_sample(page_table, q, cnk, k_new, v_new, cache_k, cache_v, bias_rows, layer, n_slots, t_new):
    bsz, n_pages = page_table.shape
    _, n_groups, grows, _ = q.shape
    _, _, page, n_heads, dh = cache_k.shape
    cols = page * HEAD_GROUP

    def page_idx(b, g, pt, r):
        return pt[b, n_pages - 1 - (g * n_slots + r)]

    def per_batch(shape):
        return pl.BlockSpec((1,) + shape, lambda b, g, pt: (b,) + (0,) * len(shape))

    in_specs = [per_batch((n_groups, grows, dh)),
                per_batch((n_groups, 1, 2 * t_new * HEAD_GROUP)),
                per_batch((2 * t_new, n_heads, dh)), per_batch((2 * t_new, n_heads, dh))]
    for _ in range(2):
        in_specs += [pl.BlockSpec((1, 1, page, n_heads, dh),
                                  functools.partial(lambda b, g, pt, r: (layer, page_idx(b, g, pt, r), 0, 0, 0), r=r))
                     for r in range(n_slots)]
    in_specs += [pl.BlockSpec((1, n_groups, 2, cols),
                              functools.partial(lambda b, g, pt, r: (page_idx(b, g, pt, r), 0, 0, 0), r=r))
                 for r in range(n_slots)]
    grid_spec = pltpu.PrefetchScalarGridSpec(
        num_scalar_prefetch=1,
        grid=(bsz, n_pages // n_slots),
        in_specs=in_specs,
        out_specs=per_batch((n_groups, grows, dh)),
        scratch_shapes=[pltpu.VMEM((n_groups, grows, 1), F32), pltpu.VMEM((n_groups, grows, 1), F32),
                        pltpu.VMEM((n_groups, grows, dh), F32), pltpu.VMEM((n_groups, 1, cols), F32)],
    )
    kern = functools.partial(_fox_sample_kernel, n_slots=n_slots, t_new=t_new)
    return pl.pallas_call(
        kern,
        grid_spec=grid_spec,
        out_shape=jax.ShapeDtypeStruct(q.shape, F32),
        compiler_params=_cparams("arbitrary", "arbitrary"),
    )(page_table, q, cnk, k_new, v_new,
      *([cache_k] * n_slots), *([cache_v] * n_slots), *([bias_rows] * n_slots))


def fox_sample_wrapper(page_table, zq, zk, zv, c_new, cache_k, cache_v, cache_lf, layer, bsz, t_new, n_slots,
                       bias_pages_per_step):
    _, pool, page, n_heads, dh = cache_k.shape
    n_groups = n_heads // HEAD_GROUP
    grows = HEAD_GROUP * t_new
    q = (zq * dh ** -0.5).reshape(bsz, t_new, n_heads, dh).transpose(0, 2, 1, 3).reshape(bsz, n_groups, grows, dh)
    c = c_new[:, :n_heads].reshape(bsz, t_new, n_groups, HEAD_GROUP)
    cnk = jnp.pad(c.transpose(0, 2, 1, 3), ((0, 0), (0, 0), (0, t_new), (0, 0)))
    cnk = cnk.reshape(bsz, n_groups, 1, 2 * t_new * HEAD_GROUP)
    pad = ((0, 0), (0, t_new), (0, 0), (0, 0))
    k_new = jnp.pad(zk.reshape(bsz, t_new, n_heads, dh), pad)
    v_new = jnp.pad(zv.reshape(bsz, t_new, n_heads, dh), pad)
    bias = fox_cache_bias(jnp.swapaxes(cache_lf, 2, 3), layer, bias_pages_per_step)
    bias_rows = bias.reshape(pool, n_groups, HEAD_GROUP, 2, page).transpose(0, 1, 3, 4, 2)
    bias_rows = bias_rows.reshape(pool, n_groups, 2, page * HEAD_GROUP)
    o = fox_sample(page_table, q, cnk, k_new, v_new, cache_k, cache_v, bias_rows, layer, n_slots, t_new)
    return o.reshape(bsz, n_heads, t_new, dh).transpose(0, 2, 1, 3).reshape(bsz * t_new, n_heads * dh)


def _layer_norm(h, g, b):
    mu = jnp.mean(h, axis=1, keepdims=True)
    d = h - mu
    var = jnp.mean(jnp.square(d), axis=1, keepdims=True)
    return d * lax.rsqrt(var + LN_EPS) * g + b


def _merge_kernel(ya_ref, yb_ref, wa_ref, wb_ref, ga_ref, gb_ref, bm_ref, o_ref):
    a = jnp.dot(ya_ref[...], wa_ref[...], preferred_element_type=F32)
    b = jnp.dot(yb_ref[...], wb_ref[...], preferred_element_type=F32)
    o = _sigmoid(ga_ref[...] + bm_ref[0:1]) * a + _sigmoid(gb_ref[...] + bm_ref[1:2]) * b
    o_ref[...] = o.astype(o_ref.dtype)


def merge_branches(ya, yb, wa, wb, z, ga_col, gb_col, b_merge, tm, tn):
    m, k = ya.shape
    n = wa.shape[1]
    return pl.pallas_call(
        _merge_kernel,
        grid=(m // tm, n // tn),
        in_specs=[pl.BlockSpec((tm, k), lambda i, j: (i, 0)),
                  pl.BlockSpec((tm, k), lambda i, j: (i, 0)),
                  pl.BlockSpec((k, tn), lambda i, j: (0, j)),
                  pl.BlockSpec((k, tn), lambda i, j: (0, j)),
                  pl.BlockSpec((tm, tn), lambda i, j: (i, ga_col + j)),
                  pl.BlockSpec((tm, tn), lambda i, j: (i, gb_col + j)),
                  pl.BlockSpec((2, tn), lambda i, j: (0, j))],
        out_specs=pl.BlockSpec((tm, tn), lambda i, j: (i, j)),
        out_shape=jax.ShapeDtypeStruct((m, n), BF16),
        compiler_params=_cparams("parallel", "parallel"),
    )(ya, yb, wa, wb, z, z, b_merge)


def _out_ln_kernel(m_ref, w_ref, x_ref, g_ref, b_ref, u_ref, ub_ref, *, alpha):
    mix = jnp.dot(m_ref[...], w_ref[...], preferred_element_type=F32)
    u = _layer_norm(alpha * x_ref[...] + mix, g_ref[...], b_ref[...])
    u_ref[...] = u
    ub_ref[...] = u.astype(BF16)


def out_proj_ln(merged, w_out, x, g, b, alpha, tm):
    m, k = merged.shape
    n = w_out.shape[1]
    row = lambda i: (i, 0)
    fixed = lambda i: (0, 0)
    return pl.pallas_call(
        functools.partial(_out_ln_kernel, alpha=alpha),
        grid=(m // tm,),
        in_specs=[pl.BlockSpec((tm, k), row), pl.BlockSpec((k, n), fixed), pl.BlockSpec((tm, n), row),
                  pl.BlockSpec((1, n), fixed), pl.BlockSpec((1, n), fixed)],
        out_specs=[pl.BlockSpec((tm, n), row), pl.BlockSpec((tm, n), row)],
        out_shape=[jax.ShapeDtypeStruct((m, n), F32), jax.ShapeDtypeStruct((m, n), BF16)],
        compiler_params=_cparams("parallel"),
    )(merged, w_out, x, g, b)


def _ple_kernel(u_ref, wg_ref, pe_ref, wp_ref, o_ref):
    gate = _sigmoid(jnp.dot(u_ref[...], wg_ref[...], preferred_element_type=F32))
    proj = jnp.dot(pe_ref[...], wp_ref[...], preferred_element_type=F32)
    o_ref[...] = (gate * proj).astype(o_ref.dtype)


def ple_term(ub, w_gate, pe, w_proj, tm, tn):
    m, k = ub.shape
    n = w_gate.shape[1]
    kp = pe.shape[1]
    return pl.pallas_call(
        _ple_kernel,
        grid=(m // tm, n // tn),
        in_specs=[pl.BlockSpec((tm, k), lambda i, j: (i, 0)),
                  pl.BlockSpec((k, tn), lambda i, j: (0, j)),
                  pl.BlockSpec((tm, kp), lambda i, j: (i, 0)),
                  pl.BlockSpec((kp, tn), lambda i, j: (0, j))],
        out_specs=pl.BlockSpec((tm, tn), lambda i, j: (i, j)),
        out_shape=jax.ShapeDtypeStruct((m, n), BF16),
        compiler_params=_cparams("parallel", "parallel"),
    )(ub, w_gate, pe, w_proj)


def _router_kernel(u_ref, wrt_ref, br_ref, comb_ref, sel_ref):
    scores = _sigmoid(_dot_nt(wrt_ref[...], u_ref[...], precision=HI))
    sel = scores + br_ref[...]
    n_exp, tm = sel.shape
    gsize = n_exp // N_GROUPS
    groups = [sel[g * gsize:(g + 1) * gsize] for g in range(N_GROUPS)]
    rows = []
    for x in groups:
        m1 = jnp.max(x, axis=0, keepdims=True)
        dup = jnp.sum(jnp.where(x == m1, 1.0, 0.0), axis=0, keepdims=True)
        m2 = jnp.max(jnp.where(x < m1, x, -jnp.inf), axis=0, keepdims=True)
        rows.append(m1 + jnp.where(dup >= 2.0, m1, m2))
    gscore = jnp.concatenate(rows, axis=0)
    gidx = lax.broadcasted_iota(jnp.int32, gscore.shape, 0)
    grank = jnp.zeros(gscore.shape, F32)
    for g in range(N_GROUPS):
        o = gscore[g:g + 1]
        grank = grank + jnp.where(o > gscore, 1.0, jnp.where(o == gscore, jnp.where(g < gidx, 1.0, 0.0), 0.0))
    masked = jnp.concatenate(
        [jnp.where(grank[g:g + 1] < TOPK_GROUPS, groups[g], NEG_INF) for g in range(N_GROUPS)], axis=0)
    sel_ref[...] = masked
    eidx = lax.broadcasted_iota(jnp.int32, masked.shape, 0)

    def count_better(e, rank):
        o = sel_ref[pl.ds(e, 1), :]
        return rank + jnp.where(o > masked, 1.0, jnp.where(o == masked, jnp.where(e < eidx, 1.0, 0.0), 0.0))

    rank = lax.fori_loop(0, n_exp, count_better, jnp.zeros(masked.shape, F32))
    w = jnp.where(rank < TOP_K, scores, 0.0)
    w = w / jnp.sum(w, axis=0, keepdims=True) * ROUTE_SCALE
    pad = jnp.zeros((comb_ref.shape[1] - n_exp, tm), F32)
    comb_ref[...] = jnp.concatenate([w, pad], axis=0).T


def router(u, w_router_t, b_router, tm):
    m, k = u.shape
    n_exp = w_router_t.shape[0]
    return pl.pallas_call(
        _router_kernel,
        grid=(m // tm,),
        in_specs=[pl.BlockSpec((tm, k), lambda i: (i, 0)),
                  pl.BlockSpec((n_exp, k), lambda i: (0, 0)),
                  pl.BlockSpec((n_exp, 1), lambda i: (0, 0))],
        out_specs=pl.BlockSpec((tm, LANES), lambda i: (i, 0)),
        out_shape=jax.ShapeDtypeStruct((m, LANES), F32),
        scratch_shapes=[pltpu.VMEM((n_exp, tm), F32)],
        compiler_params=_cparams("parallel"),
    )(u, w_router_t, b_router)


MOE_EXPERTS_PER_STEP = 2


def _moe_kernel(u_ref, comb_ref, wg_ref, wu_ref, wd_ref, o_ref):
    step = pl.program_id(1)

    @pl.when(step == 0)
    def _():
        o_ref[...] = jnp.zeros_like(o_ref)

    u = u_ref[...]
    comb = comb_ref[...]
    lane = lax.broadcasted_iota(jnp.int32, comb.shape, 1)
    n_local = wg_ref.shape[0]
    hg = [jnp.dot(u, wg_ref[j].astype(BF16), preferred_element_type=F32) for j in range(n_local)]
    hu = [jnp.dot(u, wu_ref[j].astype(BF16), preferred_element_type=F32) for j in range(n_local)]
    acc = None
    for j in range(n_local):
        e = step * n_local + j
        c = jnp.sum(jnp.where(lane == e, comb, 0.0), axis=1, keepdims=True)
        h = (_silu(hg[j]) * hu[j] * c).astype(BF16)
        y = jnp.dot(h, wd_ref[j].astype(BF16), preferred_element_type=F32)
        acc = y if acc is None else acc + y
    o_ref[...] += acc


def moe_routed(ub, comb, w_gate, w_up, w_down, tm):
    m, k = ub.shape
    n_exp, _, f = w_gate.shape
    per = MOE_EXPERTS_PER_STEP
    return pl.pallas_call(
        _moe_kernel,
        grid=(m // tm, n_exp // per),
        in_specs=[pl.BlockSpec((tm, k), lambda i, e: (i, 0)),
                  pl.BlockSpec((tm, LANES), lambda i, e: (i, 0)),
                  pl.BlockSpec((per, k, f), lambda i, e: (e, 0, 0)),
                  pl.BlockSpec((per, k, f), lambda i, e: (e, 0, 0)),
                  pl.BlockSpec((per, f, k), lambda i, e: (e, 0, 0))],
        out_specs=pl.BlockSpec((tm, k), lambda i, e: (i, 0)),
        out_shape=jax.ShapeDtypeStruct((m, k), F32),
        compiler_params=_cparams("parallel", "arbitrary"),
    )(ub, comb, w_gate, w_up, w_down)


def _final_kernel(u_ref, ub_ref, routed_ref, ple_ref, wsg_ref, wsu_ref, wsd_ref, g_ref, b_ref, y_ref, *, alpha):
    ub = ub_ref[...]
    hs = _silu(jnp.dot(ub, wsg_ref[...], preferred_element_type=F32)) * jnp.dot(
        ub, wsu_ref[...], preferred_element_type=F32)
    shared = jnp.dot(hs.astype(BF16), wsd_ref[...], preferred_element_type=F32)
    h = alpha * u_ref[...] + (routed_ref[...] + shared) + ple_ref[...].astype(F32)
    y_ref[...] = _layer_norm(h, g_ref[...], b_ref[...])


def final_ln(u, ub, routed, ple, wsg, wsu, wsd, g, b, alpha, tm):
    m, n = u.shape
    f = wsg.shape[1]
    row = lambda i: (i, 0)
    fixed = lambda i: (0, 0)
    return pl.pallas_call(
        functools.partial(_final_kernel, alpha=alpha),
        grid=(m // tm,),
        in_specs=[pl.BlockSpec((tm, n), row), pl.BlockSpec((tm, n), row), pl.BlockSpec((tm, n), row),
                  pl.BlockSpec((tm, n), row), pl.BlockSpec((n, f), fixed), pl.BlockSpec((n, f), fixed),
                  pl.BlockSpec((f, n), fixed), pl.BlockSpec((1, n), fixed), pl.BlockSpec((1, n), fixed)],
        out_specs=pl.BlockSpec((tm, n), row),
        out_shape=jax.ShapeDtypeStruct((m, n), F32),
        compiler_params=_cparams("parallel"),
    )(u, ub, routed, ple, wsg, wsu, wsd, g, b)


TM_MATMUL = 832
TM_NORM = 416
TM_ROUTER = 640
TN_IN = 1024
TN_MERGE = 512
TN_PLE = 1024
HGRN_ROWS = 128
FOX_GATE_ROWS = 256
FOX_BLK = 1024
FOX_SLOTS = 8
FOX_BIAS_PAGES = 64


def kernel(x_prompt, x_sample, p_prompt, p_sample, cache_k, cache_v, cache_lf, state_hgrn, page_table, w_in, b_fox_f, b_merge, lb_param, hgrn_norm_g, w_proj_a, w_proj_b, w_out, ln1_g, ln1_b, ln2_g, ln2_b, w_router, b_router, w_exp_gate, w_exp_up, w_exp_down, w_sh_gate, w_sh_up, w_sh_down, w_ple_proj, w_ple_gate):
    depth, d_model, _ = w_in.shape
    assert depth == 1 and x_prompt.shape[0] == 1, "single layer, single prompt sequence"
    t_p = x_prompt.shape[1]
    bsz, t_new, _ = x_sample.shape
    t_s = bsz * t_new
    n_ha, dk_a, dv_a = state_hgrn.shape[2:]
    _, pool, page, n_hb, dh_b = cache_k.shape
    assert dk_a == LANES and dv_a == LANES and dh_b == LANES and n_hb % 2 == 0
    wa_cols = n_ha * LANES
    wb_cols = n_hb * LANES
    c_qb = 4 * wa_cols
    c_fb = c_qb + 3 * wb_cols
    alpha = (2.0 * depth) ** 0.25
    l = 0

    x_all = jnp.concatenate([x_prompt.reshape(t_p, d_model), x_sample.reshape(t_s, d_model)], axis=0)
    wt_in = jnp.swapaxes(w_in, 1, 2)
    xb = x_all.astype(BF16)
    wt_gates = wt_in[l:l + 1, c_fb + n_hb:]
    wt_fb = jnp.pad(wt_in[l:l + 1, c_fb:c_fb + n_hb], ((0, 0), (0, LANES - n_hb), (0, 0)))
    z = in_proj(xb, wt_in, l, c_fb, TM_MATMUL, TN_IN)
    zg = in_proj(xb, wt_gates, 0, 2 * d_model, TM_MATMUL, TN_IN)
    zf = in_proj(xb, wt_fb, 0, LANES, TM_MATMUL, LANES)

    ng = hgrn_norm_g[l][None]
    ya_p, s_p = hgrn_prompt(z, lb_param, ng, t_p, n_ha, HGRN_ROWS)
    zs = jnp.pad(z[t_p:, :c_qb].reshape(bsz, t_new, c_qb), ((0, 0), (0, HGRN_BLK - t_new), (0, 0)))
    ya_s, s_s = hgrn_sample(zs.reshape(bsz * HGRN_BLK, c_qb), lb_param, ng, state_hgrn[l], n_ha, t_new)
    ya_s = ya_s.reshape(bsz, HGRN_BLK, wa_cols)[:, :t_new].reshape(t_s, wa_cols)

    fbias = jnp.pad(b_fox_f[l], (0, LANES - n_hb))[None]
    lf_p, c_p = fox_gate_prompt(zf, fbias, t_p, FOX_GATE_ROWS)
    lf_s, c_s = fox_gate_sample(zf[t_p:], fbias, t_new)
    q_col = c_qb // LANES
    yb_p = fox_prompt(z, c_p, t_p, n_hb, q_col, q_col + n_hb, q_col + 2 * n_hb, FOX_BLK)
    z_s = z[t_p:]
    k_s = z_s[:, c_qb + wb_cols:c_qb + 2 * wb_cols]
    v_s = z_s[:, c_qb + 2 * wb_cols:c_fb]
    yb_s = fox_sample_wrapper(page_table, z_s[:, c_qb:c_qb + wb_cols], k_s, v_s, c_s,
                              cache_k, cache_v, cache_lf, l, bsz, t_new, FOX_SLOTS, FOX_BIAS_PAGES)

    ya = jnp.concatenate([ya_p, ya_s], axis=0)
    yb = jnp.concatenate([yb_p, yb_s.astype(BF16)], axis=0)
    merged = merge_branches(ya, yb, w_proj_a[l].astype(BF16), w_proj_b[l].astype(BF16), zg,
                            0, d_model // TN_MERGE, b_merge[l], TM_MATMUL, TN_MERGE)
    u, ub = out_proj_ln(merged, w_out[l].astype(BF16), x_all, ln1_g[l][None], ln1_b[l][None], alpha, TM_NORM)

    pe = jnp.concatenate([p_prompt[l].reshape(t_p, -1), p_sample[l].reshape(t_s, -1)], axis=0).astype(BF16)
    ple = ple_term(ub, w_ple_gate[l].astype(BF16), pe, w_ple_proj[l].astype(BF16), TM_MATMUL, TN_PLE)
    comb = router(u, w_router[l].T, b_router[l][:, None], TM_ROUTER)
    routed = moe_routed(ub, comb, w_exp_gate[l], w_exp_up[l], w_exp_down[l], TM_MATMUL)
    y = final_ln(u, ub, routed, ple, w_sh_gate[l].astype(BF16), w_sh_up[l].astype(BF16),
                 w_sh_down[l].astype(BF16), ln2_g[l][None], ln2_b[l][None], alpha, TM_NORM)

    k_p = z[:t_p, c_qb + wb_cols:c_qb + 2 * wb_cols]
    v_p = z[:t_p, c_qb + 2 * wb_cols:c_fb]
    return (y[:t_p].reshape(1, t_p, d_model),
            y[t_p:].reshape(bsz, t_new, d_model),
            k_p.reshape(1, 1, t_p, n_hb, dh_b),
            v_p.reshape(1, 1, t_p, n_hb, dh_b),
            lf_p[:, :n_hb].reshape(1, 1, t_p, n_hb),
            s_p[None, None],
            k_s.reshape(1, bsz, t_new, n_hb, dh_b),
            v_s.reshape(1, bsz, t_new, n_hb, dh_b),
            lf_s[:, :n_hb].reshape(1, bsz, t_new, n_hb),
            s_s[None])
```

```python
import functools

import jax
import jax.numpy as jnp
from jax import lax
from jax.experimental import pallas as pl
from jax.experimental.pallas import tpu as pltpu

F32 = jnp.float32
BF16 = jnp.bfloat16

LANES = 128
VMEM_LIMIT = 56 * 1024 * 1024

LN_EPS = 1e-5
NEG_INF = -1e30
ROUTE_SCALE = 2.5
N_GROUPS = 8
TOPK_GROUPS = 4
TOP_K = 8
HGRN_BLK = 16
HGRN_HEADS_PER_ITER = 4

HI = lax.Precision.HIGHEST


def _cparams(*sem):
    return pltpu.CompilerParams(dimension_semantics=sem, vmem_limit_bytes=VMEM_LIMIT)


def _sigmoid(x):
    return 1.0 / (1.0 + jnp.exp(-x))


def _silu(x):
    return x * _sigmoid(x)


def _log_sigmoid(x):
    return jnp.minimum(x, 0.0) - jnp.log1p(jnp.exp(-jnp.abs(x)))


def _dot_nt(a, b, precision=None):
    return lax.dot_general(a, b, (((1,), (1,)), ((), ())), preferred_element_type=F32, precision=precision)


def _dot_tn(a, b, precision=None):
    return lax.dot_general(a, b, (((0,), (0,)), ((), ())), preferred_element_type=F32, precision=precision)


def _split3(x):
    hi = x.astype(BF16)
    rest = x - hi.astype(F32)
    mid = rest.astype(BF16)
    return hi, mid, (rest - mid.astype(F32)).astype(BF16)


def _dot_mask(mask, x):
    mb = mask.astype(BF16)
    hi, mid, lo = _split3(x)
    return (jnp.dot(mb, hi, preferred_element_type=F32) + jnp.dot(mb, mid, preferred_element_type=F32)
            + jnp.dot(mb, lo, preferred_element_type=F32))


def _dot_mask_rhs(x, mask):
    mb = mask.astype(BF16)
    hi, mid, lo = _split3(x)
    return (jnp.dot(hi, mb, preferred_element_type=F32) + jnp.dot(mid, mb, preferred_element_type=F32)
            + jnp.dot(lo, mb, preferred_element_type=F32))


def _in_proj_kernel(x_ref, wt_ref, o_ref, wb_ref):
    @pl.when(pl.program_id(1) == 0)
    def _():
        wb_ref[...] = wt_ref[0].astype(BF16)

    o_ref[...] = _dot_nt(x_ref[...], wb_ref[...])


def in_proj(xb, wt, layer, n_cols, tm, tn):
    m, k = xb.shape
    return pl.pallas_call(
        _in_proj_kernel,
        grid=(n_cols // tn, m // tm),
        in_specs=[pl.BlockSpec((tm, k), lambda j, i: (i, 0)),
                  pl.BlockSpec((1, tn, k), lambda j, i: (layer, j, 0))],
        out_specs=pl.BlockSpec((tm, tn), lambda j, i: (i, j)),
        out_shape=jax.ShapeDtypeStruct((m, n_cols), F32),
        scratch_shapes=[pltpu.VMEM((tn, k), BF16)],
        compiler_params=_cparams("arbitrary", "arbitrary"),
    )(xb, wt)


def _hgrn_head(zq, zf, zi, zg, lb, ng, st, mats, n_keys, dk):
    rows = zq.shape[0]
    q = _silu(zq) * dk ** -0.5
    logf = jnp.log(lb + (1.0 - lb) * _sigmoid(zf))
    k = (1.0 - lb) * _sigmoid(-zf)
    if n_keys < HGRN_BLK:
        valid = lax.broadcasted_iota(jnp.int32, (rows, 1), 0) % HGRN_BLK < n_keys
        logf = jnp.where(valid, logf, 0.0)
        k = jnp.where(valid, k, 0.0)
    both = jnp.dot(mats, logf, preferred_element_type=F32, precision=HI)
    b, blast = both[:rows], both[rows:]
    qt = (q * jnp.exp(b)).astype(BF16)
    kt = (k * jnp.exp(blast - b)).astype(BF16)
    vb = zi.astype(BF16)
    trow = lax.broadcasted_iota(jnp.int32, (HGRN_BLK, 1), 0)
    blocks = [slice(i * HGRN_BLK, (i + 1) * HGRN_BLK) for i in range(rows // HGRN_BLK)]
    incs = [_dot_tn(vb[sl], kt[sl]) for sl in blocks]
    states = []
    for sl, inc in zip(blocks, incs):
        states.append(st)
        st = st * jnp.exp(blast[sl.start:sl.start + 1]) + inc
    outs = [_dot_nt(qt[sl], s_in.astype(BF16)) for sl, s_in in zip(blocks, states)]
    for i, sl in enumerate(blocks):
        qb, kb, bb, vv = q[sl], k[sl], b[sl], zi[sl]
        o_blk = outs[i]
        for s in range(n_keys):
            e = jnp.exp(jnp.minimum(bb - bb[s:s + 1], 0.0))
            col = jnp.sum(qb * kb[s:s + 1] * e, axis=1, keepdims=True)
            o_blk = o_blk + jnp.where(trow >= s, col, 0.0) * vv[s:s + 1]
        outs[i] = o_blk
    o = jnp.concatenate(outs, axis=0)
    o = o * lax.rsqrt(jnp.mean(jnp.square(o), axis=1, keepdims=True) + LN_EPS)
    return o * ng * _silu(zg), st


def _block_mats(rows):
    r = lax.broadcasted_iota(jnp.int32, (rows, rows), 0)
    c = lax.broadcasted_iota(jnp.int32, (rows, rows), 1)
    same = (r // HGRN_BLK) == (c // HGRN_BLK)
    lmat = jnp.where(same & (c <= r), 1.0, 0.0).astype(F32)
    bmat = jnp.where(same, 1.0, 0.0).astype(F32)
    return jnp.concatenate([lmat, bmat], axis=0)


def _lower_bound(lbp_ref, sl):
    p = lbp_ref[:, sl]
    m = jnp.max(p, axis=0, keepdims=True)
    e = jnp.exp(p - m)
    return e[0:1] / jnp.sum(e, axis=0, keepdims=True)


def _hgrn_prompt_kernel(zq_ref, zf_ref, zi_ref, zg_ref, lbp_ref, ng_ref, y_ref, s_ref, st_ref, *, n_heads, dk):
    step = pl.program_id(0)

    @pl.when(step == 0)
    def _():
        st_ref[...] = jnp.zeros_like(st_ref)

    mats = _block_mats(zq_ref.shape[0])

    def head_group(i, carry):
        heads = [i * HGRN_HEADS_PER_ITER + j for j in range(HGRN_HEADS_PER_ITER)]
        lanes = [pl.ds(pl.multiple_of(h * LANES, LANES), LANES) for h in heads]
        args = [(zq_ref[:, sl], zf_ref[:, sl], zi_ref[:, sl], zg_ref[:, sl],
                 _lower_bound(lbp_ref, sl), ng_ref[:, sl], st_ref[h]) for h, sl in zip(heads, lanes)]
        outs = [_hgrn_head(*a, mats, HGRN_BLK, dk) for a in args]
        for h, sl, (y, st) in zip(heads, lanes, outs):
            y_ref[:, sl] = y.astype(y_ref.dtype)
            st_ref[h] = st
        return carry

    lax.fori_loop(0, n_heads // HGRN_HEADS_PER_ITER, head_group, 0)

    @pl.when(step == pl.num_programs(0) - 1)
    def _():
        for h in range(n_heads):
            s_ref[h] = st_ref[h].T


def hgrn_prompt(z, lb_param, norm_g, t_len, n_heads, rows):
    width = n_heads * LANES
    kern = functools.partial(_hgrn_prompt_kernel, n_heads=n_heads, dk=LANES)
    return pl.pallas_call(
        kern,
        grid=(t_len // rows,),
        in_specs=[pl.BlockSpec((rows, width), lambda i: (i, 0)),
                  pl.BlockSpec((rows, width), lambda i: (i, 1)),
                  pl.BlockSpec((rows, width), lambda i: (i, 2)),
                  pl.BlockSpec((rows, width), lambda i: (i, 3)),
                  pl.BlockSpec((2, width), lambda i: (0, 0)),
                  pl.BlockSpec((1, width), lambda i: (0, 0))],
        out_specs=[pl.BlockSpec((rows, width), lambda i: (i, 0)),
                   pl.BlockSpec((n_heads, LANES, LANES), lambda i: (0, 0, 0))],
        out_shape=[jax.ShapeDtypeStruct((t_len, width), BF16),
                   jax.ShapeDtypeStruct((n_heads, LANES, LANES), F32)],
        scratch_shapes=[pltpu.VMEM((n_heads, LANES, LANES), F32)],
        compiler_params=_cparams("arbitrary"),
    )(z, z, z, z, lb_param, norm_g)


def _hgrn_sample_kernel(zq_ref, zf_ref, zi_ref, zg_ref, lbp_ref, ng_ref, s0_ref, y_ref, s_ref, *, n_heads, dk, t_new):
    rows = zq_ref.shape[0]
    mats = _block_mats(rows)

    def head_group(i, carry):
        heads = [i * HGRN_HEADS_PER_ITER + j for j in range(HGRN_HEADS_PER_ITER)]
        lanes = [pl.ds(pl.multiple_of(h * LANES, LANES), LANES) for h in heads]
        args = [(zq_ref[:, sl], zf_ref[:, sl], zi_ref[:, sl], zg_ref[:, sl],
                 _lower_bound(lbp_ref, sl), ng_ref[:, sl], s0_ref[0, h].T) for h, sl in zip(heads, lanes)]
        outs = [_hgrn_head(*a, mats, t_new, dk) for a in args]
        for h, sl, (y, st) in zip(heads, lanes, outs):
            y_ref[:, sl] = y.astype(y_ref.dtype)
            s_ref[0, h] = st.T
        return carry

    lax.fori_loop(0, n_heads // HGRN_HEADS_PER_ITER, head_group, 0)


def hgrn_sample(zs, lb_param, norm_g, s0, n_heads, t_new):
    bsz = s0.shape[0]
    width = n_heads * LANES
    kern = functools.partial(_hgrn_sample_kernel, n_heads=n_heads, dk=LANES, t_new=t_new)
    return pl.pallas_call(
        kern,
        grid=(bsz,),
        in_specs=[pl.BlockSpec((HGRN_BLK, width), lambda b: (b, 0)),
                  pl.BlockSpec((HGRN_BLK, width), lambda b: (b, 1)),
                  pl.BlockSpec((HGRN_BLK, width), lambda b: (b, 2)),
                  pl.BlockSpec((HGRN_BLK, width), lambda b: (b, 3)),
                  pl.BlockSpec((2, width), lambda b: (0, 0)),
                  pl.BlockSpec((1, width), lambda b: (0, 0)),
                  pl.BlockSpec((1, n_heads, LANES, LANES), lambda b: (b, 0, 0, 0))],
        out_specs=[pl.BlockSpec((HGRN_BLK, width), lambda b: (b, 0)),
                   pl.BlockSpec((1, n_heads, LANES, LANES), lambda b: (b, 0, 0, 0))],
        out_shape=[jax.ShapeDtypeStruct((bsz * HGRN_BLK, width), BF16),
                   jax.ShapeDtypeStruct(s0.shape, F32)],
        compiler_params=_cparams("parallel"),
    )(zs, zs, zs, zs, lb_param, norm_g, s0)


def _fox_gate_prompt_kernel(zf_ref, bias_ref, lf_ref, ct_ref, carry_ref):
    @pl.when(pl.program_id(0) == 0)
    def _():
        carry_ref[...] = jnp.zeros_like(carry_ref)

    lf = _log_sigmoid(zf_ref[...] + bias_ref[...])
    lf_ref[...] = lf
    rows = lf.shape[0]
    r = lax.broadcasted_iota(jnp.int32, (rows, rows), 0)
    c = lax.broadcasted_iota(jnp.int32, (rows, rows), 1)
    lower = jnp.where(c <= r, 1.0, 0.0).astype(BF16)
    cs = _dot_mask(lower, lf) + carry_ref[...]
    ct_ref[...] = cs
    carry_ref[...] = cs[rows - 1:rows]


def fox_gate_prompt(zf, bias, t_len, rows):
    return pl.pallas_call(
        _fox_gate_prompt_kernel,
        grid=(t_len // rows,),
        in_specs=[pl.BlockSpec((rows, LANES), lambda i: (i, 0)),
                  pl.BlockSpec((1, LANES), lambda i: (0, 0))],
        out_specs=[pl.BlockSpec((rows, LANES), lambda i: (i, 0)),
                   pl.BlockSpec((rows, LANES), lambda i: (i, 0))],
        out_shape=[jax.ShapeDtypeStruct((t_len, LANES), F32),
                   jax.ShapeDtypeStruct((t_len, LANES), F32)],
        scratch_shapes=[pltpu.VMEM((1, LANES), F32)],
        compiler_params=_cparams("arbitrary"),
    )(zf, bias)


def _fox_gate_sample_kernel(zf_ref, bias_ref, lf_ref, c_ref, *, t_new):
    lf = _log_sigmoid(zf_ref[...] + bias_ref[...])
    lf_ref[...] = lf
    rows = lf.shape[0]
    r = lax.broadcasted_iota(jnp.int32, (rows, rows), 0)
    c = lax.broadcasted_iota(jnp.int32, (rows, rows), 1)
    tri = jnp.where(((r // t_new) == (c // t_new)) & (c <= r), 1.0, 0.0).astype(BF16)
    c_ref[...] = _dot_mask(tri, lf)


def fox_gate_sample(zf, bias, t_new):
    rows = zf.shape[0]
    return pl.pallas_call(
        functools.partial(_fox_gate_sample_kernel, t_new=t_new),
        grid=(1,),
        in_specs=[pl.BlockSpec((rows, LANES), lambda i: (0, 0)),
                  pl.BlockSpec((1, LANES), lambda i: (0, 0))],
        out_specs=[pl.BlockSpec((rows, LANES), lambda i: (0, 0)),
                   pl.BlockSpec((rows, LANES), lambda i: (0, 0))],
        out_shape=[jax.ShapeDtypeStruct((rows, LANES), F32),
                   jax.ShapeDtypeStruct((rows, LANES), F32)],
        compiler_params=_cparams("arbitrary"),
    )(zf, bias)


LOG2E = 1.4426950408889634
FOX_ROW_CHUNKS = 4


def _fox_prompt_kernel(q_ref, k_ref, v_ref, c_ref, o_ref, kb_ref, vb_ref, *, blk, scale):
    h = pl.program_id(0)
    i = pl.program_id(1)
    t_len = k_ref.shape[0]

    @pl.when(i == 0)
    def _():
        lane = lax.broadcasted_iota(jnp.int32, (t_len, LANES), 1)
        beta = -LOG2E * jnp.sum(jnp.where(lane == h, c_ref[...], 0.0), axis=1, keepdims=True)
        hi = beta.astype(BF16).astype(F32)
        mid = (beta - hi).astype(BF16).astype(F32)
        lo = beta - hi - mid
        aug = jnp.where(lane == 0, hi, jnp.where(lane == 1, mid, jnp.where(lane == 2, lo, 0.0)))
        kb_ref[:, :LANES] = k_ref[...].astype(BF16)
        kb_ref[:, LANES:] = aug.astype(BF16)
        vb_ref[:, :LANES] = v_ref[...].astype(BF16)
        vb_ref[:, LANES:] = jnp.where(lane == 0, 1.0, 0.0).astype(BF16)

    qlane = lax.broadcasted_iota(jnp.int32, (blk, LANES), 1)
    q = jnp.concatenate([(q_ref[...] * (scale * LOG2E)).astype(BF16),
                         jnp.where(qlane < 3, 1.0, 0.0).astype(BF16)], axis=1)

    sub = blk // FOX_ROW_CHUNKS
    qs = [q[r * sub:(r + 1) * sub] for r in range(FOX_ROW_CHUNKS)]
    row = lax.broadcasted_iota(jnp.int32, (sub, blk), 0)
    col = lax.broadcasted_iota(jnp.int32, (sub, blk), 1)

    def update(j, state, diagonal):
        off = pl.multiple_of(j * blk, blk)
        kj = kb_ref[pl.ds(off, blk), :]
        vj = vb_ref[pl.ds(off, blk), :]
        scored = [_dot_nt(qr, kj) for qr in qs]
        out = []
        for r, (s, (m, acc)) in enumerate(zip(scored, state)):
            if diagonal:
                s = jnp.where(col <= row + r * sub, s, NEG_INF)
            m_new = jnp.maximum(m, jnp.max(s, axis=1, keepdims=True))
            p = jnp.exp2(s - m_new).astype(BF16)
            out.append((m_new, jnp.exp2(m - m_new) * acc + jnp.dot(p, vj, preferred_element_type=F32)))
        return tuple(out)

    init = tuple((jnp.full((sub, 1), NEG_INF, F32), jnp.zeros((sub, 2 * LANES), F32))
                 for _ in range(FOX_ROW_CHUNKS))
    state = lax.fori_loop(0, i // 2, lambda j, st: update(2 * j + 1, update(2 * j, st, False), False), init)
    state = lax.fori_loop(2 * (i // 2), i, lambda j, st: update(j, st, False), state)
    state = update(i, state, True)
    for r, (_, acc) in enumerate(state):
        o_ref[r * sub:(r + 1) * sub, :] = (acc[:, :LANES] / acc[:, LANES:LANES + 1]).astype(o_ref.dtype)


def fox_prompt(z, c, t_len, n_heads, q_col, k_col, v_col, blk):
    kern = functools.partial(_fox_prompt_kernel, blk=blk, scale=LANES ** -0.5)
    return pl.pallas_call(
        kern,
        grid=(n_heads, t_len // blk),
        in_specs=[pl.BlockSpec((blk, LANES), lambda h, i: (i, q_col + h)),
                  pl.BlockSpec((t_len, LANES), lambda h, i: (0, k_col + h)),
                  pl.BlockSpec((t_len, LANES), lambda h, i: (0, v_col + h)),
                  pl.BlockSpec((t_len, LANES), lambda h, i: (0, 0))],
        out_specs=pl.BlockSpec((blk, LANES), lambda h, i: (i, h)),
        out_shape=jax.ShapeDtypeStruct((t_len, n_heads * LANES), BF16),
        scratch_shapes=[pltpu.VMEM((t_len, 2 * LANES), BF16), pltpu.VMEM((t_len, 2 * LANES), BF16)],
        compiler_params=_cparams("arbitrary", "arbitrary"),
    )(z, z, z, c)


HEAD_GROUP = 8


def _cache_bias_kernel(lf_ref, o_ref):
    _, n_pages, n_heads, page = lf_ref.shape
    ks = lax.broadcasted_iota(jnp.int32, (page, 2 * page), 0)
    kj = lax.broadcasted_iota(jnp.int32, (page, 2 * page), 1)
    later = jnp.where(ks > kj, 1.0, jnp.where(kj >= page, 1.0, 0.0)).astype(BF16)
    lf = lf_ref[0].reshape(n_pages * n_heads, page)
    o_ref[...] = _dot_mask_rhs(lf, later).reshape(n_pages, n_heads, 2 * page)


def fox_cache_bias(cache_lf_t, layer, pages_per_step):
    depth, pool, n_heads, page = cache_lf_t.shape
    return pl.pallas_call(
        _cache_bias_kernel,
        grid=(pool // pages_per_step,),
        in_specs=[pl.BlockSpec((1, pages_per_step, n_heads, page), lambda i: (layer, i, 0, 0))],
        out_specs=pl.BlockSpec((pages_per_step, n_heads, 2 * page), lambda i: (i, 0, 0)),
        out_shape=jax.ShapeDtypeStruct((pool, n_heads, 2 * page), F32),
        compiler_params=_cparams("parallel"),
    )(cache_lf_t)


def _fox_sample_kernel(pt_ref, q_ref, cnk_ref, kn_ref, vn_ref, *rest, n_slots, t_new):
    k_refs, v_refs, b_refs = rest[:n_slots], rest[n_slots:2 * n_slots], rest[2 * n_slots:3 * n_slots]
    o_ref = rest[3 * n_slots]
    m_ref, l_ref, acc_ref, carry_ref = rest[3 * n_slots + 1:]
    step = pl.program_id(1)
    n_groups = q_ref.shape[1]
    grows = q_ref.shape[2]
    page = k_refs[0].shape[2]

    def own_head(n_keys):
        r = lax.broadcasted_iota(jnp.int32, (grows, n_keys * HEAD_GROUP), 0)
        c = lax.broadcasted_iota(jnp.int32, (grows, n_keys * HEAD_GROUP), 1)
        return r, c, (c % HEAD_GROUP) == (r // t_new)

    def tile(ref, lead, g):
        x = ref[lead + (slice(None), slice(g * HEAD_GROUP, (g + 1) * HEAD_GROUP), slice(None))]
        return x.reshape(x.shape[0] * HEAD_GROUP, x.shape[2]).astype(BF16)

    @pl.when(step == 0)
    def _():
        r, c, same = own_head(2 * t_new)
        visible = same & ((c // HEAD_GROUP) <= (r % t_new))
        for g in range(n_groups):
            s = _dot_nt(q_ref[0, g].astype(BF16), tile(kn_ref, (0,), g))
            s = jnp.where(visible, s - cnk_ref[0, g], NEG_INF)
            m0 = jnp.max(s, axis=1, keepdims=True)
            p0 = jnp.exp(s - m0)
            m_ref[g] = m0
            l_ref[g] = jnp.sum(p0, axis=1, keepdims=True)
            acc_ref[g] = jnp.dot(p0.astype(BF16), tile(vn_ref, (0,), g), preferred_element_type=F32)
        carry_ref[...] = jnp.zeros_like(carry_ref)

    _, _, same = own_head(page)
    other_head = jnp.where(same, 0.0, NEG_INF)
    for g in range(n_groups):
        qg = q_ref[0, g].astype(BF16)
        carry = carry_ref[g]
        parts = []
        for r in range(n_slots):
            bias = other_head + (carry + b_refs[r][0, g, 0:1, :])
            parts.append(_dot_nt(qg, tile(k_refs[r], (0, 0), g)) + bias)
            carry = carry + b_refs[r][0, g, 1:2, :]
        carry_ref[g] = carry
        m = m_ref[g]
        m_new = m
        for s in parts:
            m_new = jnp.maximum(m_new, jnp.max(s, axis=1, keepdims=True))
        alpha = jnp.exp(m - m_new)
        l = alpha * l_ref[g]
        acc = alpha * acc_ref[g]
        for r, s in enumerate(parts):
            p = jnp.exp(s - m_new)
            l = l + jnp.sum(p, axis=1, keepdims=True)
            acc = acc + jnp.dot(p.astype(BF16), tile(v_refs[r], (0, 0), g), preferred_element_type=F32)
        m_ref[g], l_ref[g], acc_ref[g] = m_new, l, acc

    @pl.when(step == pl.num_programs(1) - 1)
    def _():
        for g in range(n_groups):
            o_ref[0, g] = acc_ref[g] / l_ref[g]


def fox_sample(page_table, q, cnk, k_new, v_new, cache_k, cache_v, bias_rows, layer, n_slots, t_new):
    bsz, n_pages = page_table.shape
    _, n_groups, grows, _ = q.shape
    _, _, page, n_heads, dh = cache_k.shape
    cols = page * HEAD_GROUP

    def page_idx(b, g, pt, r):
        return pt[b, n_pages - 1 - (g * n_slots + r)]

    def per_batch(shape):
        return pl.BlockSpec((1,) + shape, lambda b, g, pt: (b,) + (0,) * len(shape))

    in_specs = [per_batch((n_groups, grows, dh)),
                per_batch((n_groups, 1, 2 * t_new * HEAD_GROUP)),
                per_batch((2 * t_new, n_heads, dh)), per_batch((2 * t_new, n_heads, dh))]
    for _ in range(2):
        in_specs += [pl.BlockSpec((1, 1, page, n_heads, dh),
                                  functools.partial(lambda b, g, pt, r: (layer, page_idx(b, g, pt, r), 0, 0, 0), r=r))
                     for r in range(n_slots)]
    in_specs += [pl.BlockSpec((1, n_groups, 2, cols),
                              functools.partial(lambda b, g, pt, r: (page_idx(b, g, pt, r), 0, 0, 0), r=r))
                 for r in range(n_slots)]
    grid_spec = pltpu.PrefetchScalarGridSpec(
        num_scalar_prefetch=1,
        grid=(bsz, n_pages // n_slots),
        in_specs=in_specs,
        out_specs=per_batch((n_groups, grows, dh)),
        scratch_shapes=[pltpu.VMEM((n_groups, grows, 1), F32), pltpu.VMEM((n_groups, grows, 1), F32),
                        pltpu.VMEM((n_groups, grows, dh), F32), pltpu.VMEM((n_groups, 1, cols), F32)],
    )
    kern = functools.partial(_fox_sample_kernel, n_slots=n_slots, t_new=t_new)
    return pl.pallas_call(
        kern,
        grid_spec=grid_spec,
        out_shape=jax.ShapeDtypeStruct(q.shape, F32),
        compiler_params=_cparams("arbitrary", "arbitrary"),
    )(page_table, q, cnk, k_new, v_new,
      *([cache_k] * n_slots), *([cache_v] * n_slots), *([bias_rows] * n_slots))


def fox_sample_wrapper(page_table, zq, zk, zv, c_new, cache_k, cache_v, cache_lf, layer, bsz, t_new, n_slots,
                       bias_pages_per_step):
    _, pool, page, n_heads, dh = cache_k.shape
    n_groups = n_heads // HEAD_GROUP
    grows = HEAD_GROUP * t_new
    q = (zq * dh ** -0.5).reshape(bsz, t_new, n_heads, dh).transpose(0, 2, 1, 3).reshape(bsz, n_groups, grows, dh)
    c = c_new[:, :n_heads].reshape(bsz, t_new, n_groups, HEAD_GROUP)
    cnk = jnp.pad(c.transpose(0, 2, 1, 3), ((0, 0), (0, 0), (0, t_new), (0, 0)))
    cnk = cnk.reshape(bsz, n_groups, 1, 2 * t_new * HEAD_GROUP)
    pad = ((0, 0), (0, t_new), (0, 0), (0, 0))
    k_new = jnp.pad(zk.reshape(bsz, t_new, n_heads, dh), pad)
    v_new = jnp.pad(zv.reshape(bsz, t_new, n_heads, dh), pad)
    bias = fox_cache_bias(jnp.swapaxes(cache_lf, 2, 3), layer, bias_pages_per_step)
    bias_rows = bias.reshape(pool, n_groups, HEAD_GROUP, 2, page).transpose(0, 1, 3, 4, 2)
    bias_rows = bias_rows.reshape(pool, n_groups, 2, page * HEAD_GROUP)
    o = fox_sample(page_table, q, cnk, k_new, v_new, cache_k, cache_v, bias_rows, layer, n_slots, t_new)
    return o.reshape(bsz, n_heads, t_new, dh).transpose(0, 2, 1, 3).reshape(bsz * t_new, n_heads * dh)


def _layer_norm(h, g, b):
    mu = jnp.mean(h, axis=1, keepdims=True)
    d = h - mu
    var = jnp.mean(jnp.square(d), axis=1, keepdims=True)
    return d * lax.rsqrt(var + LN_EPS) * g + b


def _merge_kernel(ya_ref, yb_ref, wa_ref, wb_ref, ga_ref, gb_ref, bm_ref, o_ref):
    a = jnp.dot(ya_ref[...], wa_ref[...], preferred_element_type=F32)
    b = jnp.dot(yb_ref[...], wb_ref[...], preferred_element_type=F32)
    o = _sigmoid(ga_ref[...] + bm_ref[0:1]) * a + _sigmoid(gb_ref[...] + bm_ref[1:2]) * b
    o_ref[...] = o.astype(o_ref.dtype)


def merge_branches(ya, yb, wa, wb, z, ga_col, gb_col, b_merge, tm, tn):
    m, k = ya.shape
    n = wa.shape[1]
    return pl.pallas_call(
        _merge_kernel,
        grid=(m // tm, n // tn),
        in_specs=[pl.BlockSpec((tm, k), lambda i, j: (i, 0)),
                  pl.BlockSpec((tm, k), lambda i, j: (i, 0)),
                  pl.BlockSpec((k, tn), lambda i, j: (0, j)),
                  pl.BlockSpec((k, tn), lambda i, j: (0, j)),
                  pl.BlockSpec((tm, tn), lambda i, j: (i, ga_col + j)),
                  pl.BlockSpec((tm, tn), lambda i, j: (i, gb_col + j)),
                  pl.BlockSpec((2, tn), lambda i, j: (0, j))],
        out_specs=pl.BlockSpec((tm, tn), lambda i, j: (i, j)),
        out_shape=jax.ShapeDtypeStruct((m, n), BF16),
        compiler_params=_cparams("parallel", "parallel"),
    )(ya, yb, wa, wb, z, z, b_merge)


def _out_ln_kernel(m_ref, w_ref, x_ref, g_ref, b_ref, u_ref, ub_ref, *, alpha):
    mix = jnp.dot(m_ref[...], w_ref[...], preferred_element_type=F32)
    u = _layer_norm(alpha * x_ref[...] + mix, g_ref[...], b_ref[...])
    u_ref[...] = u
    ub_ref[...] = u.astype(BF16)


def out_proj_ln(merged, w_out, x, g, b, alpha, tm):
    m, k = merged.shape
    n = w_out.shape[1]
    row = lambda i: (i, 0)
    fixed = lambda i: (0, 0)
    return pl.pallas_call(
        functools.partial(_out_ln_kernel, alpha=alpha),
        grid=(m // tm,),
        in_specs=[pl.BlockSpec((tm, k), row), pl.BlockSpec((k, n), fixed), pl.BlockSpec((tm, n), row),
                  pl.BlockSpec((1, n), fixed), pl.BlockSpec((1, n), fixed)],
        out_specs=[pl.BlockSpec((tm, n), row), pl.BlockSpec((tm, n), row)],
        out_shape=[jax.ShapeDtypeStruct((m, n), F32), jax.ShapeDtypeStruct((m, n), BF16)],
        compiler_params=_cparams("parallel"),
    )(merged, w_out, x, g, b)


def _ple_kernel(u_ref, wg_ref, pe_ref, wp_ref, o_ref):
    gate = _sigmoid(jnp.dot(u_ref[...], wg_ref[...], preferred_element_type=F32))
    proj = jnp.dot(pe_ref[...], wp_ref[...], preferred_element_type=F32)
    o_ref[...] = (gate * proj).astype(o_ref.dtype)


def ple_term(ub, w_gate, pe, w_proj, tm, tn):
    m, k = ub.shape
    n = w_gate.shape[1]
    kp = pe.shape[1]
    return pl.pallas_call(
        _ple_kernel,
        grid=(m // tm, n // tn),
        in_specs=[pl.BlockSpec((tm, k), lambda i, j: (i, 0)),
                  pl.BlockSpec((k, tn), lambda i, j: (0, j)),
                  pl.BlockSpec((tm, kp), lambda i, j: (i, 0)),
                  pl.BlockSpec((kp, tn), lambda i, j: (0, j))],
        out_specs=pl.BlockSpec((tm, tn), lambda i, j: (i, j)),
        out_shape=jax.ShapeDtypeStruct((m, n), BF16),
        compiler_params=_cparams("parallel", "parallel"),
    )(ub, w_gate, pe, w_proj)


def _router_kernel(u_ref, wrt_ref, br_ref, comb_ref, sel_ref):
    scores = _sigmoid(_dot_nt(wrt_ref[...], u_ref[...], precision=HI))
    sel = scores + br_ref[...]
    n_exp, tm = sel.shape
    gsize = n_exp // N_GROUPS
    groups = [sel[g * gsize:(g + 1) * gsize] for g in range(N_GROUPS)]
    rows = []
    for x in groups:
        m1 = jnp.max(x, axis=0, keepdims=True)
        dup = jnp.sum(jnp.where(x == m1, 1.0, 0.0), axis=0, keepdims=True)
        m2 = jnp.max(jnp.where(x < m1, x, -jnp.inf), axis=0, keepdims=True)
        rows.append(m1 + jnp.where(dup >= 2.0, m1, m2))
    gscore = jnp.concatenate(rows, axis=0)
    gidx = lax.broadcasted_iota(jnp.int32, gscore.shape, 0)
    grank = jnp.zeros(gscore.shape, F32)
    for g in range(N_GROUPS):
        o = gscore[g:g + 1]
        grank = grank + jnp.where(o > gscore, 1.0, jnp.where(o == gscore, jnp.where(g < gidx, 1.0, 0.0), 0.0))
    masked = jnp.concatenate(
        [jnp.where(grank[g:g + 1] < TOPK_GROUPS, groups[g], NEG_INF) for g in range(N_GROUPS)], axis=0)
    sel_ref[...] = masked
    eidx = lax.broadcasted_iota(jnp.int32, masked.shape, 0)

    def count_better(e, rank):
        o = sel_ref[pl.ds(e, 1), :]
        return rank + jnp.where(o > masked, 1.0, jnp.where(o == masked, jnp.where(e < eidx, 1.0, 0.0), 0.0))

    rank = lax.fori_loop(0, n_exp, count_better, jnp.zeros(masked.shape, F32))
    w = jnp.where(rank < TOP_K, scores, 0.0)
    w = w / jnp.sum(w, axis=0, keepdims=True) * ROUTE_SCALE
    pad = jnp.zeros((comb_ref.shape[1] - n_exp, tm), F32)
    comb_ref[...] = jnp.concatenate([w, pad], axis=0).T


def router(u, w_router_t, b_router, tm):
    m, k = u.shape
    n_exp = w_router_t.shape[0]
    return pl.pallas_call(
        _router_kernel,
        grid=(m // tm,),
        in_specs=[pl.BlockSpec((tm, k), lambda i: (i, 0)),
                  pl.BlockSpec((n_exp, k), lambda i: (0, 0)),
                  pl.BlockSpec((n_exp, 1), lambda i: (0, 0))],
        out_specs=pl.BlockSpec((tm, LANES), lambda i: (i, 0)),
        out_shape=jax.ShapeDtypeStruct((m, LANES), F32),
        scratch_shapes=[pltpu.VMEM((n_exp, tm), F32)],
        compiler_params=_cparams("parallel"),
    )(u, w_router_t, b_router)


MOE_EXPERTS_PER_STEP = 2


def _moe_kernel(u_ref, comb_ref, wg_ref, wu_ref, wd_ref, o_ref):
    step = pl.program_id(1)

    @pl.when(step == 0)
    def _():
        o_ref[...] = jnp.zeros_like(o_ref)

    u = u_ref[...]
    comb = comb_ref[...]
    lane = lax.broadcasted_iota(jnp.int32, comb.shape, 1)
    n_local = wg_ref.shape[0]
    hg = [jnp.dot(u, wg_ref[j].astype(BF16), preferred_element_type=F32) for j in range(n_local)]
    hu = [jnp.dot(u, wu_ref[j].astype(BF16), preferred_element_type=F32) for j in range(n_local)]
    acc = None
    for j in range(n_local):
        e = step * n_local + j
        c = jnp.sum(jnp.where(lane == e, comb, 0.0), axis=1, keepdims=True)
        h = (_silu(hg[j]) * hu[j] * c).astype(BF16)
        y = jnp.dot(h, wd_ref[j].astype(BF16), preferred_element_type=F32)
        acc = y if acc is None else acc + y
    o_ref[...] += acc


def moe_routed(ub, comb, w_gate, w_up, w_down, tm):
    m, k = ub.shape
    n_exp, _, f = w_gate.shape
    per = MOE_EXPERTS_PER_STEP
    return pl.pallas_call(
        _moe_kernel,
        grid=(m // tm, n_exp // per),
        in_specs=[pl.BlockSpec((tm, k), lambda i, e: (i, 0)),
                  pl.BlockSpec((tm, LANES), lambda i, e: (i, 0)),
                  pl.BlockSpec((per, k, f), lambda i, e: (e, 0, 0)),
                  pl.BlockSpec((per, k, f), lambda i, e: (e, 0, 0)),
                  pl.BlockSpec((per, f, k), lambda i, e: (e, 0, 0))],
        out_specs=pl.BlockSpec((tm, k), lambda i, e: (i, 0)),
        out_shape=jax.ShapeDtypeStruct((m, k), F32),
        compiler_params=_cparams("parallel", "arbitrary"),
    )(ub, comb, w_gate, w_up, w_down)


def _final_kernel(u_ref, ub_ref, routed_ref, ple_ref, wsg_ref, wsu_ref, wsd_ref, g_ref, b_ref, y_ref, *, alpha):
    ub = ub_ref[...]
    hs = _silu(jnp.dot(ub, wsg_ref[...], preferred_element_type=F32)) * jnp.dot(
        ub, wsu_ref[...], preferred_element_type=F32)
    shared = jnp.dot(hs.astype(BF16), wsd_ref[...], preferred_element_type=F32)
    h = alpha * u_ref[...] + (routed_ref[...] + shared) + ple_ref[...].astype(F32)
    y_ref[...] = _layer_norm(h, g_ref[...], b_ref[...])


def final_ln(u, ub, routed, ple, wsg, wsu, wsd, g, b, alpha, tm):
    m, n = u.shape
    f = wsg.shape[1]
    row = lambda i: (i, 0)
    fixed = lambda i: (0, 0)
    return pl.pallas_call(
        functools.partial(_final_kernel, alpha=alpha),
        grid=(m // tm,),
        in_specs=[pl.BlockSpec((tm, n), row), pl.BlockSpec((tm, n), row), pl.BlockSpec((tm, n), row),
                  pl.BlockSpec((tm, n), row), pl.BlockSpec((n, f), fixed), pl.BlockSpec((n, f), fixed),
                  pl.BlockSpec((f, n), fixed), pl.BlockSpec((1, n), fixed), pl.BlockSpec((1, n), fixed)],
        out_specs=pl.BlockSpec((tm, n), row),
        out_shape=jax.ShapeDtypeStruct((m, n), F32),
        compiler_params=_cparams("parallel"),
    )(u, ub, routed, ple, wsg, wsu, wsd, g, b)


TM_MATMUL = 832
TM_NORM = 416
TM_ROUTER = 640
TN_IN = 1024
TN_MERGE = 512
TN_PLE = 1024
HGRN_ROWS = 128
FOX_GATE_ROWS = 256
FOX_BLK = 1024
FOX_SLOTS = 8
FOX_BIAS_PAGES = 64


def kernel(x_prompt, x_sample, p_prompt, p_sample, cache_k, cache_v, cache_lf, state_hgrn, page_table, w_in, b_fox_f, b_merge, lb_param, hgrn_norm_g, w_proj_a, w_proj_b, w_out, ln1_g, ln1_b, ln2_g, ln2_b, w_router, b_router, w_exp_gate, w_exp_up, w_exp_down, w_sh_gate, w_sh_up, w_sh_down, w_ple_proj, w_ple_gate):
    depth, d_model, _ = w_in.shape
    assert depth == 1 and x_prompt.shape[0] == 1, "single layer, single prompt sequence"
    t_p = x_prompt.shape[1]
    bsz, t_new, _ = x_sample.shape
    t_s = bsz * t_new
    n_ha, dk_a, dv_a = state_hgrn.shape[2:]
    _, pool, page, n_hb, dh_b = cache_k.shape
    assert dk_a == LANES and dv_a == LANES and dh_b == LANES and n_hb % 2 == 0
    wa_cols = n_ha * LANES
    wb_cols = n_hb * LANES
    c_qb = 4 * wa_cols
    c_fb = c_qb + 3 * wb_cols
    alpha = (2.0 * depth) ** 0.25
    l = 0

    x_all = jnp.concatenate([x_prompt.reshape(t_p, d_model), x_sample.reshape(t_s, d_model)], axis=0)
    wt_in = jnp.swapaxes(w_in, 1, 2)
    xb = x_all.astype(BF16)
    wt_gates = wt_in[l:l + 1, c_fb + n_hb:]
    wt_fb = jnp.pad(wt_in[l:l + 1, c_fb:c_fb + n_hb], ((0, 0), (0, LANES - n_hb), (0, 0)))
    z = in_proj(xb, wt_in, l, c_fb, TM_MATMUL, TN_IN)
    zg = in_proj(xb, wt_gates, 0, 2 * d_model, TM_MATMUL, TN_IN)
    zf = in_proj(xb, wt_fb, 0, LANES, TM_MATMUL, LANES)

    ng = hgrn_norm_g[l][None]
    ya_p, s_p = hgrn_prompt(z, lb_param, ng, t_p, n_ha, HGRN_ROWS)
    zs = jnp.pad(z[t_p:, :c_qb].reshape(bsz, t_new, c_qb), ((0, 0), (0, HGRN_BLK - t_new), (0, 0)))
    ya_s, s_s = hgrn_sample(zs.reshape(bsz * HGRN_BLK, c_qb), lb_param, ng, state_hgrn[l], n_ha, t_new)
    ya_s = ya_s.reshape(bsz, HGRN_BLK, wa_cols)[:, :t_new].reshape(t_s, wa_cols)

    fbias = jnp.pad(b_fox_f[l], (0, LANES - n_hb))[None]
    lf_p, c_p = fox_gate_prompt(zf, fbias, t_p, FOX_GATE_ROWS)
    lf_s, c_s = fox_gate_sample(zf[t_p:], fbias, t_new)
    q_col = c_qb // LANES
    yb_p = fox_prompt(z, c_p, t_p, n_hb, q_col, q_col + n_hb, q_col + 2 * n_hb, FOX_BLK)
    z_s = z[t_p:]
    k_s = z_s[:, c_qb + wb_cols:c_qb + 2 * wb_cols]
    v_s = z_s[:, c_qb + 2 * wb_cols:c_fb]
    yb_s = fox_sample_wrapper(page_table, z_s[:, c_qb:c_qb + wb_cols], k_s, v_s, c_s,
                              cache_k, cache_v, cache_lf, l, bsz, t_new, FOX_SLOTS, FOX_BIAS_PAGES)

    ya = jnp.concatenate([ya_p, ya_s], axis=0)
    yb = jnp.concatenate([yb_p, yb_s.astype(BF16)], axis=0)
    merged = merge_branches(ya, yb, w_proj_a[l].astype(BF16), w_proj_b[l].astype(BF16), zg,
                            0, d_model // TN_MERGE, b_merge[l], TM_MATMUL, TN_MERGE)
    u, ub = out_proj_ln(merged, w_out[l].astype(BF16), x_all, ln1_g[l][None], ln1_b[l][None], alpha, TM_NORM)

    pe = jnp.concatenate([p_prompt[l].reshape(t_p, -1), p_sample[l].reshape(t_s, -1)], axis=0).astype(BF16)
    ple = ple_term(ub, w_ple_gate[l].astype(BF16), pe, w_ple_proj[l].astype(BF16), TM_MATMUL, TN_PLE)
    comb = router(u, w_router[l].T, b_router[l][:, None], TM_ROUTER)
    routed = moe_routed(ub, comb, w_exp_gate[l], w_exp_up[l], w_exp_down[l], TM_MATMUL)
    y = final_ln(u, ub, routed, ple, w_sh_gate[l].astype(BF16), w_sh_up[l].astype(BF16),
                 w_sh_down[l].astype(BF16), ln2_g[l][None], ln2_b[l][None], alpha, TM_NORM)

    k_p = z[:t_p, c_qb + wb_cols:c_qb + 2 * wb_cols]
    v_p = z[:t_p, c_qb + 2 * wb_cols:c_fb]
    return (y[:t_p].reshape(1, t_p, d_model),
            y[t_p:].reshape(bsz, t_new, d_model),
            k_p.reshape(1, 1, t_p, n_hb, dh_b),
            v_p.reshape(1, 1, t_p, n_hb, dh_b),
            lf_p[:, :n_hb].reshape(1, 1, t_p, n_hb),
            s_p[None, None],
            k_s.reshape(1, bsz, t_new, n_hb, dh_b),
            v_s.reshape(1, bsz, t_new, n_hb, dh_b),
            lf_s[:, :n_hb].reshape(1, bsz, t_new, n_hb),
            s_s[None])
```
